```python
import jax, jax.numpy as jnp
from jax import lax
import numpy as np

D_MODEL = 1024
BATCH = 16
SEQ = 2048
DEPTH = 2

CHUNK = 64
HEAD_DIM = 128
EPS = 1e-6
N_HEADS_A = 4
A_WIDTH = N_HEADS_A * HEAD_DIM
IDX_HEADS = 8
IDX_DIM = 64
IDX_TOPK_MAX = 256
N_HEADS_B = 4
GLA_DK = 64
GLA_DV = 128
GLA_GATE_RANK = 16
GLA_TAU = 16.0
B_WIDTH = N_HEADS_B * GLA_DV
N_HEADS_C = 8
C_WIDTH = N_HEADS_C * HEAD_DIM
SB_QBLOCK = 128
ROPE_THETA = 500000.0
ROPE_FRACTION = 4
D_FF = 4 * D_MODEL
N_EVEN = (DEPTH + 1) // 2
N_ODD = DEPTH // 2
AB_SIZES = (
    A_WIDTH, A_WIDTH, A_WIDTH,
    IDX_HEADS * IDX_DIM, IDX_DIM, IDX_HEADS,
    N_HEADS_B * GLA_DK, N_HEADS_B * GLA_DK,
    B_WIDTH,
    GLA_GATE_RANK,
    B_WIDTH,
)
AB_PROJ = sum(AB_SIZES)

kernel_name = "hybrid_dsa_gla_stickbreak_encoder"


def _split(a, sizes):
    out, o = [], 0
    for s in sizes:
        out.append(a[..., o:o + s])
        o += s
    return out


def rmsnorm(x, g):
    xf = x.astype(jnp.float32)
    y = xf * lax.rsqrt(jnp.mean(xf * xf, axis=-1, keepdims=True) + EPS)
    return (y * g.astype(jnp.float32)).astype(x.dtype)


def rope_partial(x, pos):
    d = x.shape[-1]
    rot = d // ROPE_FRACTION
    half = rot // 2
    inv = jnp.power(ROPE_THETA, -jnp.arange(half, dtype=jnp.float32) * 2.0 / rot)
    ang = pos.astype(jnp.float32)[:, None] * inv[None, :]
    cos = jnp.cos(ang)[None, :, None, :]
    sin = jnp.sin(ang)[None, :, None, :]
    xf = x.astype(jnp.float32)
    x1, x2, rest = xf[..., :half], xf[..., half:rot], xf[..., rot:]
    out = jnp.concatenate([x1 * cos - x2 * sin, x2 * cos + x1 * sin, rest], axis=-1)
    return out.astype(x.dtype)


def dsa_attention(q, k, v, iq, ik, iw):
    bsz, s_len, n_h, d_h = q.shape
    topk = min(IDX_TOPK_MAX, s_len // 4)
    n_blk = s_len // CHUNK
    key_pos = jnp.arange(s_len)

    def blk_fn(args):
        c, qb, iqb, iwb = args
        logits = jnp.einsum('bthd,bsd->bths', iqb, ik).astype(jnp.float32) * IDX_DIM ** -0.5
        score = jnp.einsum('bths,bth->bts', jax.nn.relu(logits),
                           iwb.astype(jnp.float32) * IDX_HEADS ** -0.5)
        limit = (c + 1) * CHUNK
        score = jnp.where((key_pos < limit)[None, None, :], score, -jnp.inf)
        _, idx = lax.top_k(score, topk)
        valid = idx < limit
        ks = jax.vmap(lambda a, i: a[i])(k, idx)
        vs = jax.vmap(lambda a, i: a[i])(v, idx)
        s = jnp.einsum('bthd,btkhd->bthk', qb, ks).astype(jnp.float32) * HEAD_DIM ** -0.5
        s = jnp.where(valid[:, :, None, :], s, -jnp.inf)
        p = jax.nn.softmax(s, axis=-1)
        return jnp.einsum('bthk,btkhd->bthd', p.astype(v.dtype), vs)

    def to_blocks(a):
        return a.reshape(bsz, n_blk, CHUNK, *a.shape[2:]).swapaxes(0, 1)

    out = lax.map(blk_fn, (jnp.arange(n_blk), to_blocks(q), to_blocks(iq), to_blocks(iw)))
    return out.swapaxes(0, 1).reshape(bsz, s_len, n_h, d_h)


def gla_chunked(q, k, v, lg):
    bsz, s_len, n_h, dk = q.shape
    dv = v.shape[-1]
    n_c = s_len // CHUNK
    causal = jnp.tril(jnp.ones((CHUNK, CHUNK), dtype=bool))

    def chunks(a):
        return a.astype(jnp.float32).reshape(bsz, n_c, CHUNK, n_h, a.shape[-1]).transpose(1, 0, 3, 2, 4)

    def step(state, inp):
        qc, kc, vc, gc = inp
        b = jnp.cumsum(gc, axis=2)
        o_inter = jnp.einsum('bhtd,bhde->bhte', qc * jnp.exp(b), state)
        diff = b[:, :, :, None, :] - b[:, :, None, :, :]
        decay = jnp.exp(jnp.where(causal[:, :, None], diff, -jnp.inf))
        att = jnp.einsum('bhtd,bhsd,bhtsd->bhts', qc, kc, decay)
        o = o_inter + jnp.einsum('bhts,bhse->bhte', att, vc)
        b_last = b[:, :, -1:, :]
        state = (jnp.exp(b_last[:, :, 0, :])[..., None] * state
                 + jnp.einsum('bhsd,bhse->bhde', kc * jnp.exp(b_last - b), vc))
        return state, o

    s0 = jnp.zeros((bsz, n_h, dk, dv), jnp.float32)
    _, o = lax.scan(step, s0, (chunks(q), chunks(k), chunks(v), chunks(lg)))
    return o.transpose(1, 0, 3, 2, 4).reshape(bsz, s_len, n_h, dv)


def stick_breaking(q, k, v):
    bsz, s_len, n_h, d_h = q.shape
    n_blk = s_len // SB_QBLOCK
    key_pos = jnp.arange(s_len)

    def blk_fn(args):
        i, qb = args
        z = jnp.einsum('bthd,bshd->bhts', qb, k).astype(jnp.float32) * d_h ** -0.5
        t_pos = i * SB_QBLOCK + jnp.arange(SB_QBLOCK)
        mask = key_pos[None, :] < t_pos[:, None]
        lo = jnp.where(mask, jax.nn.log_sigmoid(-z), 0.0)
        rc = lax.cumsum(lo, axis=3, reverse=True)
        log_a = jax.nn.log_sigmoid(z) + (rc - lo)
        a = jnp.where(mask, jnp.exp(log_a), 0.0)
        return jnp.einsum('bhts,bshd->bthd', a.astype(v.dtype), v)

    qb = q.reshape(bsz, n_blk, SB_QBLOCK, n_h, d_h).swapaxes(0, 1)
    out = lax.map(blk_fn, (jnp.arange(n_blk), qb))
    return out.swapaxes(0, 1).reshape(bsz, s_len, n_h, d_h)


def mixer_ab(n, pos, w_in, gq, gk, w_gate_up, b_gate, g_gla, w_out):
    bsz, s_len, _ = n.shape
    proj = n @ w_in
    (qa, ka, va, iq, ik, iw, qb, kb, vb, g_low, og) = _split(proj, AB_SIZES)
    qa = rope_partial(rmsnorm(qa.reshape(bsz, s_len, N_HEADS_A, HEAD_DIM), gq), pos)
    ka = rope_partial(rmsnorm(ka.reshape(bsz, s_len, N_HEADS_A, HEAD_DIM), gk), pos)
    va = va.reshape(bsz, s_len, N_HEADS_A, HEAD_DIM)
    iq = rope_partial(iq.reshape(bsz, s_len, IDX_HEADS, IDX_DIM), pos)
    ik = rope_partial(ik[:, :, None, :], pos)[:, :, 0, :]
    oa = dsa_attention(qa, ka, va, iq, ik, iw).reshape(bsz, s_len, A_WIDTH)
    qb = qb.reshape(bsz, s_len, N_HEADS_B, GLA_DK) * GLA_DK ** -0.5
    kb = kb.reshape(bsz, s_len, N_HEADS_B, GLA_DK)
    vb = vb.reshape(bsz, s_len, N_HEADS_B, GLA_DV)
    gate = (g_low @ w_gate_up + b_gate).astype(jnp.float32)
    lg = (jax.nn.log_sigmoid(gate) / GLA_TAU).reshape(bsz, s_len, N_HEADS_B, GLA_DK)
    ob = gla_chunked(qb, kb, vb, lg).astype(n.dtype)
    ob = rmsnorm(ob, g_gla).reshape(bsz, s_len, B_WIDTH) * jax.nn.silu(og)
    return jnp.concatenate([oa, ob], axis=-1) @ w_out


def mixer_c(n, w_in, gq, gk, w_out):
    bsz, s_len, _ = n.shape
    q, k, v = _split(n @ w_in, (C_WIDTH, C_WIDTH, C_WIDTH))
    q = rmsnorm(q.reshape(bsz, s_len, N_HEADS_C, HEAD_DIM), gq)
    k = rmsnorm(k.reshape(bsz, s_len, N_HEADS_C, HEAD_DIM), gk)
    v = v.reshape(bsz, s_len, N_HEADS_C, HEAD_DIM)
    return stick_breaking(q, k, v).reshape(bsz, s_len, C_WIDTH) @ w_out


def squared_relu_mlp(n, w_up, w_down):
    return jnp.square(jax.nn.relu(n @ w_up)) @ w_down


def setup_inputs(seed: int = 0) -> dict:
    key = jax.random.key(seed)
    ks = jax.random.split(key, 16)
    f32 = jnp.float32

    def w(k, shape, fan_in):
        return jax.random.normal(k, shape, f32) * fan_in ** -0.5

    def gain(k, shape):
        return 1.0 + 0.02 * jax.random.normal(k, shape, f32)

    return {
        "x": jax.random.normal(ks[0], (BATCH, SEQ, D_MODEL), f32),
        "g_mix": gain(ks[1], (DEPTH, D_MODEL)),
        "g_ffn": gain(ks[2], (DEPTH, D_MODEL)),
        "w_in_ab": w(ks[3], (N_EVEN, D_MODEL, AB_PROJ), D_MODEL),
        "gq_a": gain(ks[4], (N_EVEN, HEAD_DIM)),
        "gk_a": gain(ks[5], (N_EVEN, HEAD_DIM)),
        "w_gate_up": w(ks[6], (N_EVEN, GLA_GATE_RANK, N_HEADS_B * GLA_DK), GLA_GATE_RANK),
        "b_gate": 0.1 * jax.random.normal(ks[7], (N_EVEN, N_HEADS_B * GLA_DK), f32),
        "g_gla": gain(ks[8], (N_EVEN, GLA_DV)),
        "w_out_ab": w(ks[9], (N_EVEN, A_WIDTH + B_WIDTH, D_MODEL), A_WIDTH + B_WIDTH),
        "w_in_c": w(ks[10], (N_ODD, D_MODEL, 3 * C_WIDTH), D_MODEL),
        "gq_c": gain(ks[11], (N_ODD, HEAD_DIM)),
        "gk_c": gain(ks[12], (N_ODD, HEAD_DIM)),
        "w_out_c": w(ks[13], (N_ODD, C_WIDTH, D_MODEL), C_WIDTH),
        "w_up": w(ks[14], (DEPTH, D_MODEL, D_FF), D_MODEL),
        "w_down": w(ks[15], (DEPTH, D_FF, D_MODEL), D_FF),
    }


def reference(x, g_mix, g_ffn, w_in_ab, gq_a, gk_a, w_gate_up, b_gate, g_gla, w_out_ab,
              w_in_c, gq_c, gk_c, w_out_c, w_up, w_down):
    pos = jnp.arange(x.shape[1])
    h = x
    for layer in range(DEPTH):
        n = rmsnorm(h, g_mix[layer])
        if layer % 2 == 0:
            i = layer // 2
            h = h + mixer_ab(n, pos, w_in_ab[i], gq_a[i], gk_a[i], w_gate_up[i],
                             b_gate[i], g_gla[i], w_out_ab[i])
        else:
            i = layer // 2
            h = h + mixer_c(n, w_in_c[i], gq_c[i], gk_c[i], w_out_c[i])
        h = h + squared_relu_mlp(rmsnorm(h, g_ffn[layer]), w_up[layer], w_down[layer])
    return h
```

```python
import functools

import numpy as np
import jax
import jax.numpy as jnp
from jax import lax
from jax.experimental import pallas as pl
from jax.experimental.pallas import tpu as pltpu

F32 = jnp.float32
BF16 = jnp.bfloat16

D_MODEL = 1024
CHUNK = 64
HEAD_DIM = 128
EPS = 1e-6
N_HEADS_A = 4
A_WIDTH = N_HEADS_A * HEAD_DIM
IDX_HEADS = 8
IDX_DIM = 64
IDX_TOPK_MAX = 256
N_HEADS_B = 4
GLA_DK = 64
GLA_DV = 128
GLA_GATE_RANK = 16
GLA_TAU = 16.0
B_WIDTH = N_HEADS_B * GLA_DV
N_HEADS_C = 8
C_WIDTH = N_HEADS_C * HEAD_DIM
ROPE_THETA = 500000.0
ROPE_FRACTION = 4
D_FF = 4 * D_MODEL
AB_SIZES = (A_WIDTH, A_WIDTH, A_WIDTH, IDX_HEADS * IDX_DIM, IDX_DIM, IDX_HEADS,
            N_HEADS_B * GLA_DK, N_HEADS_B * GLA_DK, B_WIDTH, GLA_GATE_RANK, B_WIDTH)

LANES = 128
VMEM_LIMIT = 48 * 1024 * 1024

W512 = 512
COL_QA, COL_KA, COL_VA, COL_IQ, COL_QB, COL_KB, COL_VB, COL_OG = range(8)
COL128_IK, COL128_IW, COL128_GLOW = 32, 33, 34
AB_PAD = 35 * LANES

INT_MIN = np.int32(-2 ** 31)


def _cparams(sem):
    return pltpu.CompilerParams(dimension_semantics=sem, vmem_limit_bytes=VMEM_LIMIT)


def _rms(x, g):
    return x * lax.rsqrt(jnp.mean(x * x, axis=-1, keepdims=True) + EPS) * g


def _log_sigmoid(x):
    return jnp.minimum(x, 0.0) - jnp.log1p(jnp.exp(-jnp.abs(x)))


def _dot_t(a, b):
    return lax.dot_general(a, b, (((1,), (1,)), ((), ())), preferred_element_type=F32)


def _norm_matmul_kernel(x_ref, g_ref, w_ref, o_ref, xn_ref):
    @pl.when(pl.program_id(1) == 0)
    def _():
        xn_ref[...] = _rms(x_ref[...], g_ref[...]).astype(BF16)

    o_ref[...] = jnp.dot(xn_ref[...], w_ref[...], preferred_element_type=F32).astype(o_ref.dtype)


def _norm_matmul(x, g, w, tm, tn):
    m, d = x.shape
    n = w.shape[1]
    return pl.pallas_call(
        _norm_matmul_kernel,
        grid=(m // tm, n // tn),
        in_specs=[pl.BlockSpec((tm, d), lambda i, j: (i, 0)),
                  pl.BlockSpec((1, d), lambda i, j: (0, 0)),
                  pl.BlockSpec((d, tn), lambda i, j: (0, j))],
        out_specs=pl.BlockSpec((tm, tn), lambda i, j: (i, j)),
        out_shape=jax.ShapeDtypeStruct((m, n), F32),
        scratch_shapes=[pltpu.VMEM((tm, d), BF16)],
        compiler_params=_cparams(("parallel", "arbitrary")),
        name="norm_matmul",
    )(x, g.reshape(1, d), w)


def _proj_mlp_kernel(n_a, h_ref, *refs):
    a_refs = refs[:n_a]
    wo_refs = refs[n_a:2 * n_a]
    g_ref, wu_ref, wd_ref, o_ref, acc_ref, xn_ref = refs[2 * n_a:]
    f = pl.program_id(1)

    @pl.when(f == 0)
    def _():
        h1 = h_ref[...]
        for a_ref, wo_ref in zip(a_refs, wo_refs):
            h1 = h1 + jnp.dot(a_ref[...], wo_ref[...], preferred_element_type=F32)
        acc_ref[...] = h1
        xn_ref[...] = _rms(h1, g_ref[...]).astype(BF16)

    u = jnp.dot(xn_ref[...], wu_ref[...], preferred_element_type=F32)
    u = jnp.square(jnp.maximum(u, 0.0)).astype(BF16)
    acc_ref[...] += jnp.dot(u, wd_ref[...], preferred_element_type=F32)

    @pl.when(f == pl.num_programs(1) - 1)
    def _():
        o_ref[...] = acc_ref[...]


def _proj_mlp(h, a_list, wo_list, g, w_up, w_down, tm, tf):
    m, d = h.shape
    ff = w_up.shape[1]
    n_a = len(a_list)
    in_specs = [pl.BlockSpec((tm, d), lambda i, f: (i, 0))]
    in_specs += [pl.BlockSpec((tm, a.shape[1]), lambda i, f: (i, 0)) for a in a_list]
    in_specs += [pl.BlockSpec(w.shape, lambda i, f: (0, 0)) for w in wo_list]
    in_specs += [pl.BlockSpec((1, d), lambda i, f: (0, 0)),
                 pl.BlockSpec((d, tf), lambda i, f: (0, f)),
                 pl.BlockSpec((tf, d), lambda i, f: (f, 0))]
    return pl.pallas_call(
        functools.partial(_proj_mlp_kernel, n_a),
        grid=(m // tm, ff // tf),
        in_specs=in_specs,
        out_specs=pl.BlockSpec((tm, d), lambda i, f: (i, 0)),
        out_shape=jax.ShapeDtypeStruct((m, d), F32),
        scratch_shapes=[pltpu.VMEM((tm, d), F32), pltpu.VMEM((tm, d), BF16)],
        compiler_params=_cparams(("parallel", "arbitrary")),
        name="proj_mlp",
    )(h, *a_list, *wo_list, g.reshape(1, d), w_up, w_down)


def _rope(x, tab_ref, half):
    out = []
    for s in range(x.shape[1] // LANES):
        xs = x[:, s * LANES:(s + 1) * LANES]
        out.append(xs * tab_ref[0]
                   + pltpu.roll(xs, LANES - half, 1) * tab_ref[1]
                   + pltpu.roll(xs, half, 1) * tab_ref[2])
    return out


def _head_rms(x, g):
    out = []
    for s in range(x.shape[1] // HEAD_DIM):
        out.append(_rms(x[:, s * HEAD_DIM:(s + 1) * HEAD_DIM], g))
    return jnp.concatenate(out, axis=1)


def _prep_a_kernel(qa_ref, ka_ref, va_ref, iq_ref, ik_ref, iw_ref, gq_ref, gk_ref, ta_ref, ti_ref,
                   qo_ref, ko_ref, vo_ref, iqo_ref, iko_ref, iwo_ref):
    q = _head_rms(qa_ref[...], gq_ref[...])
    k = _head_rms(ka_ref[...], gk_ref[...])
    qo_ref[...] = (jnp.concatenate(_rope(q, ta_ref, HEAD_DIM // ROPE_FRACTION // 2), axis=1)
                   * HEAD_DIM ** -0.5).astype(BF16)
    ko_ref[...] = jnp.concatenate(_rope(k, ta_ref, HEAD_DIM // ROPE_FRACTION // 2), axis=1).astype(BF16)
    vo_ref[...] = va_ref[...].astype(BF16)
    ihalf = IDX_DIM // ROPE_FRACTION // 2
    iqo_ref[...] = (jnp.concatenate(_rope(iq_ref[...], ti_ref, ihalf), axis=1) * IDX_DIM ** -0.5).astype(BF16)
    ik = _rope(ik_ref[...], ti_ref, ihalf)[0]
    ik = jnp.where(lax.broadcasted_iota(jnp.int32, ik.shape, 1) < IDX_DIM, ik, 0.0)
    iko_ref[...] = jnp.concatenate([ik, pltpu.roll(ik, IDX_DIM, 1)], axis=1).astype(BF16)
    iwo_ref[...] = iw_ref[...] * IDX_HEADS ** -0.5


def _prep_a(proj, gq, gk, tab_a, tab_i, ts):
    bsz, s_len, _ = proj.shape

    def col(c, w):
        return pl.BlockSpec((None, ts, w), lambda s, b, c=c: (b, s, c))

    def out(w):
        return pl.BlockSpec((None, ts, w), lambda s, b: (b, s, 0))

    tab_spec = pl.BlockSpec((3, ts, LANES), lambda s, b: (0, s, 0))
    g_spec = pl.BlockSpec((1, HEAD_DIM), lambda s, b: (0, 0))
    sd = jax.ShapeDtypeStruct
    return pl.pallas_call(
        _prep_a_kernel,
        grid=(s_len // ts, bsz),
        in_specs=[col(COL_QA, W512), col(COL_KA, W512), col(COL_VA, W512), col(COL_IQ, W512),
                  col(COL128_IK, LANES), col(COL128_IW, LANES), g_spec, g_spec, tab_spec, tab_spec],
        out_specs=[out(W512), out(W512), out(W512), out(W512), out(2 * LANES), out(LANES)],
        out_shape=[sd((bsz, s_len, W512), BF16), sd((bsz, s_len, W512), BF16), sd((bsz, s_len, W512), BF16),
                   sd((bsz, s_len, W512), BF16), sd((bsz, s_len, 2 * LANES), BF16),
                   sd((bsz, s_len, LANES), F32)],
        compiler_params=_cparams(("parallel", "parallel")),
        name="prep_a",
    )(proj, proj, proj, proj, proj, proj, gq.reshape(1, HEAD_DIM), gk.reshape(1, HEAD_DIM), tab_a, tab_i)


def _rope_tables(s_len, period, reps):
    rot = period // ROPE_FRACTION
    half = rot // 2
    inv = jnp.power(ROPE_THETA, -jnp.arange(half, dtype=F32) * 2.0 / rot)
    ang = jnp.arange(s_len).astype(F32)[:, None] * inv[None, :]
    cos, sin = jnp.cos(ang), jnp.sin(ang)
    pad = jnp.zeros((s_len, period - rot), F32)
    zero = jnp.zeros((s_len, half), F32)
    c = jnp.concatenate([cos, cos, pad + 1.0], axis=1)
    s1 = jnp.concatenate([-sin, zero, pad], axis=1)
    s2 = jnp.concatenate([zero, sin, pad], axis=1)
    return jnp.stack([jnp.tile(t, (1, reps)) for t in (c, s1, s2)])


N_BUCKETS = 4


def _dsa_body(lp, topk, c, q_ref, k_ref, v_ref, iq_ref, ik_ref, iw_ref, o_ref, key_ref):
    tq = q_ref.shape[0]
    limit = (c + 1) * CHUNK
    col = lax.broadcasted_iota(jnp.int32, (tq, lp), 1)
    adm = col < limit
    iw = iw_ref[...]
    score = jnp.zeros((tq, lp), F32)
    for h in range(IDX_HEADS):
        pair, sub = h // 2, h % 2
        logit = _dot_t(iq_ref[:, pair * LANES:(pair + 1) * LANES],
                       ik_ref[0:lp, sub * LANES:(sub + 1) * LANES])
        score = score + jnp.maximum(logit, 0.0) * iw[:, h:h + 1]
    bits = pltpu.bitcast(score, jnp.int32)
    key = bits ^ ((bits >> 31) & jnp.int32(0x7FFFFFFF))
    key_ref[:, 0:lp] = jnp.where(adm, key, INT_MIN)

    def count(pred):
        return jnp.sum(jnp.where(pred, 1.0, 0.0), axis=-1, keepdims=True)

    def vbody(it, lo):
        cand = lo + lax.shift_left(jnp.int32(1), 31 - it)
        ok = count(key_ref[:, 0:lp] >= cand) >= topk
        return jnp.where(ok, cand, lo)

    v = lax.fori_loop(0, 32, vbody, jnp.full((tq, 1), INT_MIN, jnp.int32))
    key = key_ref[:, 0:lp]
    gt = key > v
    eq = key == v
    need = topk - count(gt)
    nbits = int(lp).bit_length()

    def jbody(it, jb):
        cand = jb + lax.shift_left(jnp.int32(1), nbits - 1 - it)
        few = count(eq & (col < cand)) < need
        return jnp.where(few, cand, jb)

    jb = lax.fori_loop(0, nbits, jbody, jnp.zeros((tq, 1), jnp.int32))
    sel = (gt | (eq & (col <= jb))) & adm

    for h in range(N_HEADS_A):
        hs = slice(h * HEAD_DIM, (h + 1) * HEAD_DIM)
        s = _dot_t(q_ref[:, hs], k_ref[0:lp, hs])
        s = jnp.where(sel, s, -jnp.inf)
        p = jnp.exp(s - jnp.max(s, axis=-1, keepdims=True))
        l = jnp.sum(p, axis=-1, keepdims=True)
        o = jnp.dot(p.astype(BF16), v_ref[0:lp, hs], preferred_element_type=F32)
        o_ref[:, hs] = (o / l).astype(o_ref.dtype)


def _dsa_kernel(topk, q_ref, k_ref, v_ref, iq_ref, ik_ref, iw_ref, o_ref, key_ref):
    c = pl.program_id(1)
    n_c = pl.num_programs(1)
    s_len = k_ref.shape[0]
    per = s_len // CHUNK // N_BUCKETS
    for bk in range(N_BUCKETS):
        @pl.when(c // per == bk)
        def _(bk=bk):
            _dsa_body((bk + 1) * (s_len // N_BUCKETS), topk, c,
                      q_ref, k_ref, v_ref, iq_ref, ik_ref, iw_ref, o_ref, key_ref)
    del n_c


def _dsa(qa, ka, va, iq, ik2, iw):
    bsz, s_len, _ = qa.shape
    topk = min(IDX_TOPK_MAX, s_len // 4)

    def qspec(w):
        return pl.BlockSpec((None, CHUNK, w), lambda b, c: (b, c, 0))

    def kspec(w):
        return pl.BlockSpec((None, s_len, w), lambda b, c: (b, 0, 0))

    return pl.pallas_call(
        functools.partial(_dsa_kernel, topk),
        grid=(bsz, s_len // CHUNK),
        in_specs=[qspec(W512), kspec(W512), kspec(W512), qspec(W512), kspec(2 * LANES), qspec(LANES)],
        out_specs=qspec(W512),
        out_shape=jax.ShapeDtypeStruct((bsz, s_len, A_WIDTH), BF16),
        scratch_shapes=[pltpu.VMEM((CHUNK, s_len), jnp.int32)],
        compiler_params=_cparams(("parallel", "arbitrary")),
        name="dsa",
    )(qa, ka, va, iq, ik2, iw)


GLA_ROWS = 256


def _gla_kernel(q_ref, k_ref, v_ref, og_ref, gl_ref, wg_ref, bg_ref, gn_ref, o_ref, st_ref):
    @pl.when(pl.program_id(1) == 0)
    def _():
        st_ref[...] = jnp.zeros_like(st_ref)

    gate = jnp.dot(gl_ref[...], wg_ref[...], preferred_element_type=F32,
                   precision=lax.Precision.HIGHEST) + bg_ref[...]
    lg = _log_sigmoid(gate) * (1.0 / GLA_TAU)
    row = lax.broadcasted_iota(jnp.int32, (CHUNK, CHUNK), 0)
    colm = lax.broadcasted_iota(jnp.int32, (CHUNK, CHUNK), 1)
    causal = row >= colm
    tril = jnp.where(causal, 1.0, 0.0)
    mid = CHUNK // 2
    for ci in range(q_ref.shape[0] // CHUNK):
        rs = slice(ci * CHUNK, (ci + 1) * CHUNK)
        for h in range(N_HEADS_B):
            cs = slice(h * LANES, (h + 1) * LANES)
            b = jnp.dot(tril, lg[rs, cs], preferred_element_type=F32, precision=lax.Precision.HIGHEST)
            q = q_ref[rs, cs] * GLA_DK ** -0.5
            k = k_ref[rs, cs]
            v = v_ref[rs, cs].astype(BF16)
            st = st_ref[h]
            o = _dot_t((q * jnp.exp(b)).astype(BF16), st.astype(BF16))
            bm = b[mid:mid + 1, :]
            att = _dot_t((q * jnp.exp(b - bm)).astype(BF16), (k * jnp.exp(bm - b)).astype(BF16))
            att = jnp.where(causal, att, 0.0)
            o = o + jnp.dot(att.astype(BF16), v, preferred_element_type=F32)
            bl = b[CHUNK - 1:CHUNK, :]
            kd = (k * jnp.exp(bl - b)).astype(BF16)
            st_ref[h] = st * jnp.exp(bl) + lax.dot_general(
                v, kd, (((0,), (0,)), ((), ())), preferred_element_type=F32)
            og = og_ref[rs, cs]
            o_ref[rs, cs] = (_rms(o, gn_ref[...]) * (og * jax.nn.sigmoid(og))).astype(o_ref.dtype)


def _gla(proj, wg, bg, gn):
    bsz, s_len, _ = proj.shape

    def col(c, w):
        return pl.BlockSpec((None, GLA_ROWS, w), lambda b, s, c=c: (b, s, c))

    return pl.pallas_call(
        _gla_kernel,
        grid=(bsz, s_len // GLA_ROWS),
        in_specs=[col(COL_QB, W512), col(COL_KB, W512), col(COL_VB, W512), col(COL_OG, W512),
                  col(COL128_GLOW, LANES),
                  pl.BlockSpec((LANES, W512), lambda b, s: (0, 0)),
                  pl.BlockSpec((1, W512), lambda b, s: (0, 0)),
                  pl.BlockSpec((1, GLA_DV), lambda b, s: (0, 0))],
        out_specs=pl.BlockSpec((None, GLA_ROWS, B_WIDTH), lambda b, s: (b, s, 0)),
        out_shape=jax.ShapeDtypeStruct((bsz, s_len, B_WIDTH), BF16),
        scratch_shapes=[pltpu.VMEM((N_HEADS_B, GLA_DV, LANES), F32)],
        compiler_params=_cparams(("parallel", "arbitrary")),
        name="gla",
    )(proj, proj, proj, proj, proj, wg, bg, gn.reshape(1, GLA_DV))


SB_T = 128


def _sb_kernel(q_ref, k_ref, v_ref, gq_ref, gk_ref, o_ref, kn_ref, vb_ref, acc_ref, car_ref):
    i = pl.program_id(2)

    @pl.when(i == 0)
    def _():
        kn_ref[...] = _rms(k_ref[...], gk_ref[...]).astype(BF16)
        vb_ref[...] = v_ref[...].astype(BF16)

    q = (_rms(q_ref[...], gq_ref[...]) * HEAD_DIM ** -0.5).astype(BF16)
    row = lax.broadcasted_iota(jnp.int32, (SB_T, SB_T), 0)
    colm = lax.broadcasted_iota(jnp.int32, (SB_T, SB_T), 1)
    ones = jnp.ones((SB_T, SB_T), BF16)
    tri = jnp.where(row >= colm, 1.0, 0.0).astype(BF16)
    half = jnp.concatenate([tri, ones], axis=1)
    cum_w = jnp.concatenate([half, half], axis=0)
    strict = colm < row

    def block(j, diag):
        start = pl.multiple_of(j * SB_T, SB_T)
        kj = kn_ref[pl.ds(start, SB_T), :]
        vj = vb_ref[pl.ds(start, SB_T), :]
        z = _dot_t(q, kj)
        lo = _log_sigmoid(-z)
        if diag:
            lo = jnp.where(strict, lo, 0.0)
        lo_hi = lo.astype(BF16)
        lo_lo = (lo - lo_hi.astype(F32)).astype(BF16)
        rt = jnp.dot(jnp.concatenate([lo_hi, lo_lo], axis=1), cum_w, preferred_element_type=F32)
        a = jnp.exp(z + rt[:, 0:SB_T] + car_ref[...])
        if diag:
            a = jnp.where(strict, a, 0.0)
        acc_ref[...] += jnp.dot(a.astype(BF16), vj, preferred_element_type=F32)
        car_ref[...] += rt[:, SB_T:2 * SB_T]

    acc_ref[...] = jnp.zeros_like(acc_ref)
    car_ref[...] = jnp.zeros_like(car_ref)
    block(i, True)

    def body(t, carry):
        block(i - 1 - t, False)
        return carry

    lax.fori_loop(0, i, body, 0)
    o_ref[...] = acc_ref[...].astype(o_ref.dtype)


def _stick_breaking(qkv, gq, gk):
    bsz, s_len, _ = qkv.shape
    g_spec = pl.BlockSpec((1, HEAD_DIM), lambda b, h, i: (0, 0))
    return pl.pallas_call(
        _sb_kernel,
        grid=(bsz, N_HEADS_C, s_len // SB_T),
        in_specs=[pl.BlockSpec((None, SB_T, HEAD_DIM), lambda b, h, i: (b, i, h)),
                  pl.BlockSpec((None, s_len, HEAD_DIM), lambda b, h, i: (b, 0, N_HEADS_C + h)),
                  pl.BlockSpec((None, s_len, HEAD_DIM), lambda b, h, i: (b, 0, 2 * N_HEADS_C + h)),
                  g_spec, g_spec],
        out_specs=pl.BlockSpec((None, SB_T, HEAD_DIM), lambda b, h, i: (b, i, h)),
        out_shape=jax.ShapeDtypeStruct((bsz, s_len, C_WIDTH), BF16),
        scratch_shapes=[pltpu.VMEM((s_len, HEAD_DIM), BF16), pltpu.VMEM((s_len, HEAD_DIM), BF16),
                        pltpu.VMEM((SB_T, HEAD_DIM), F32), pltpu.VMEM((SB_T, SB_T), F32)],
        compiler_params=_cparams(("parallel", "parallel", "arbitrary")),
        name="stick_breaking",
    )(qkv, qkv, qkv, gq.reshape(1, HEAD_DIM), gk.reshape(1, HEAD_DIM))


def _pad_heads(w, n_heads, width):
    lead = w.shape[:-1]
    d = w.shape[-1] // n_heads
    w = w.reshape(*lead, n_heads, d)
    w = jnp.pad(w, [(0, 0)] * len(lead) + [(0, 0), (0, width - d)])
    return w.reshape(*lead, n_heads * width)


def _pad_cols(w, width):
    return jnp.pad(w, [(0, 0)] * (w.ndim - 1) + [(0, width - w.shape[-1])])


def _layer0_weights(w_in):
    parts, o = [], 0
    for s in AB_SIZES:
        parts.append(w_in[:, o:o + s])
        o += s
    qa, ka, va, iq, ik, iw, qb, kb, vb, glow, og = parts
    cols = [qa, ka, va, iq, _pad_heads(qb, N_HEADS_B, LANES), _pad_heads(kb, N_HEADS_B, LANES), vb, og,
            _pad_cols(ik, LANES), _pad_cols(iw, LANES), _pad_cols(glow, LANES)]
    return jnp.concatenate(cols, axis=1).astype(BF16)


def kernel(x, g_mix, g_ffn, w_in_ab, gq_a, gk_a, w_gate_up, b_gate, g_gla, w_out_ab,
           w_in_c, gq_c, gk_c, w_out_c, w_up, w_down):
    bsz, s_len, d = x.shape
    m = bsz * s_len
    h = x.reshape(m, d)

    proj = _norm_matmul(h, g_mix[0], _layer0_weights(w_in_ab[0]), tm=1024, tn=896)
    proj = proj.reshape(bsz, s_len, AB_PAD)
    tab_a = _rope_tables(s_len, HEAD_DIM, 1)
    tab_i = _rope_tables(s_len, IDX_DIM, LANES // IDX_DIM)
    qa, ka, va, iq, ik2, iw = _prep_a(proj, gq_a[0], gk_a[0], tab_a, tab_i, ts=512)
    oa = _dsa(qa, ka, va, iq, ik2, iw)
    wg = jnp.pad(_pad_heads(w_gate_up[0], N_HEADS_B, LANES), ((0, LANES - GLA_GATE_RANK), (0, 0)))
    bg = _pad_heads(b_gate[0], N_HEADS_B, LANES).reshape(1, W512)
    ob = _gla(proj, wg, bg, g_gla[0])
    w_o = w_out_ab[0].astype(BF16)
    h = _proj_mlp(h, [oa.reshape(m, A_WIDTH), ob.reshape(m, B_WIDTH)], [w_o[:A_WIDTH], w_o[A_WIDTH:]],
                  g_ffn[0], w_up[0].astype(BF16), w_down[0].astype(BF16), tm=512, tf=512)

    qkv = _norm_matmul(h, g_mix[1], w_in_c[0].astype(BF16), tm=1024, tn=1024)
    oc = _stick_breaking(qkv.reshape(bsz, s_len, 3 * C_WIDTH), gq_c[0], gk_c[0])
    h = _proj_mlp(h, [oc.reshape(m, C_WIDTH)], [w_out_c[0].astype(BF16)],
                  g_ffn[1], w_up[1].astype(BF16), w_down[1].astype(BF16), tm=512, tf=512)
    return h.reshape(bsz, s_len, d)
```

```python
import functools

import numpy as np
import jax
import jax.numpy as jnp
from jax import lax
from jax.experimental import pallas as pl
from jax.experimental.pallas import tpu as pltpu

F32 = jnp.float32
BF16 = jnp.bfloat16

D_MODEL = 1024
CHUNK = 64
HEAD_DIM = 128
EPS = 1e-6
N_HEADS_A = 4
A_WIDTH = N_HEADS_A * HEAD_DIM
IDX_HEADS = 8
IDX_DIM = 64
IDX_TOPK_MAX = 256
N_HEADS_B = 4
GLA_DK = 64
GLA_DV = 128
GLA_GATE_RANK = 16
GLA_TAU = 16.0
B_WIDTH = N_HEADS_B * GLA_DV
N_HEADS_C = 8
C_WIDTH = N_HEADS_C * HEAD_DIM
ROPE_THETA = 500000.0
ROPE_FRACTION = 4
D_FF = 4 * D_MODEL
AB_SIZES = (A_WIDTH, A_WIDTH, A_WIDTH, IDX_HEADS * IDX_DIM, IDX_DIM, IDX_HEADS,
            N_HEADS_B * GLA_DK, N_HEADS_B * GLA_DK, B_WIDTH, GLA_GATE_RANK, B_WIDTH)

LANES = 128
VMEM_LIMIT = 48 * 1024 * 1024

W512 = 512
COL_QA, COL_KA, COL_VA, COL_IQ, COL_QB, COL_KB, COL_VB, COL_OG = range(8)
COL128_IK, COL128_IW, COL128_GLOW = 32, 33, 34
AB_PAD = 35 * LANES

INT_MIN = np.int32(-2 ** 31)


def _cparams(sem):
    return pltpu.CompilerParams(dimension_semantics=sem, vmem_limit_bytes=VMEM_LIMIT)


def _rms(x, g):
    return x * lax.rsqrt(jnp.mean(x * x, axis=-1, keepdims=True) + EPS) * g


def _log_sigmoid(x):
    return jnp.minimum(x, 0.0) - jnp.log1p(jnp.exp(-jnp.abs(x)))


def _dot_t(a, b):
    return lax.dot_general(a, b, (((1,), (1,)), ((), ())), preferred_element_type=F32)


def _norm_matmul_kernel(x_ref, g_ref, w_ref, o_ref, xn_ref):
    @pl.when(pl.program_id(1) == 0)
    def _():
        xn_ref[...] = _rms(x_ref[...], g_ref[...]).astype(BF16)

    o_ref[...] = jnp.dot(xn_ref[...], w_ref[...], preferred_element_type=F32).astype(o_ref.dtype)


def _norm_matmul(x, g, w, tm, tn):
    m, d = x.shape
    n = w.shape[1]
    return pl.pallas_call(
        _norm_matmul_kernel,
        grid=(m // tm, n // tn),
        in_specs=[pl.BlockSpec((tm, d), lambda i, j: (i, 0)),
                  pl.BlockSpec((1, d), lambda i, j: (0, 0)),
                  pl.BlockSpec((d, tn), lambda i, j: (0, j))],
        out_specs=pl.BlockSpec((tm, tn), lambda i, j: (i, j)),
        out_shape=jax.ShapeDtypeStruct((m, n), F32),
        scratch_shapes=[pltpu.VMEM((tm, d), BF16)],
        compiler_params=_cparams(("parallel", "arbitrary")),
        name="norm_matmul",
    )(x, g.reshape(1, d), w)


def _proj_mlp_kernel(n_a, h_ref, *refs):
    a_refs = refs[:n_a]
    wo_refs = refs[n_a:2 * n_a]
    g_ref, wu_ref, wd_ref, o_ref, acc_ref, xn_ref = refs[2 * n_a:]
    f = pl.program_id(1)

    @pl.when(f == 0)
    def _():
        h1 = h_ref[...]
        for a_ref, wo_ref in zip(a_refs, wo_refs):
            h1 = h1 + jnp.dot(a_ref[...], wo_ref[...], preferred_element_type=F32)
        acc_ref[...] = h1
        xn_ref[...] = _rms(h1, g_ref[...]).astype(BF16)

    u = jnp.dot(xn_ref[...], wu_ref[...], preferred_element_type=F32)
    u = jnp.square(jnp.maximum(u, 0.0)).astype(BF16)
    acc_ref[...] += jnp.dot(u, wd_ref[...], preferred_element_type=F32)

    @pl.when(f == pl.num_programs(1) - 1)
    def _():
        o_ref[...] = acc_ref[...]


def _proj_mlp(h, a_list, wo_list, g, w_up, w_down, tm, tf):
    m, d = h.shape
    ff = w_up.shape[1]
    n_a = len(a_list)
    in_specs = [pl.BlockSpec((tm, d), lambda i, f: (i, 0))]
    in_specs += [pl.BlockSpec((tm, a.shape[1]), lambda i, f: (i, 0)) for a in a_list]
    in_specs += [pl.BlockSpec(w.shape, lambda i, f: (0, 0)) for w in wo_list]
    in_specs += [pl.BlockSpec((1, d), lambda i, f: (0, 0)),
                 pl.BlockSpec((d, tf), lambda i, f: (0, f)),
                 pl.BlockSpec((tf, d), lambda i, f: (f, 0))]
    return pl.pallas_call(
        functools.partial(_proj_mlp_kernel, n_a),
        grid=(m // tm, ff // tf),
        in_specs=in_specs,
        out_specs=pl.BlockSpec((tm, d), lambda i, f: (i, 0)),
        out_shape=jax.ShapeDtypeStruct((m, d), F32),
        scratch_shapes=[pltpu.VMEM((tm, d), F32), pltpu.VMEM((tm, d), BF16)],
        compiler_params=_cparams(("parallel", "arbitrary")),
        name="proj_mlp",
    )(h, *a_list, *wo_list, g.reshape(1, d), w_up, w_down)


def _rope(x, tab_ref, half):
    out = []
    for s in range(x.shape[1] // LANES):
        xs = x[:, s * LANES:(s + 1) * LANES]
        out.append(xs * tab_ref[0]
                   + pltpu.roll(xs, LANES - half, 1) * tab_ref[1]
                   + pltpu.roll(xs, half, 1) * tab_ref[2])
    return out


def _head_rms(x, g):
    out = []
    for s in range(x.shape[1] // HEAD_DIM):
        out.append(_rms(x[:, s * HEAD_DIM:(s + 1) * HEAD_DIM], g))
    return jnp.concatenate(out, axis=1)


DSA_T = 256
IDX_K = 4 * IDX_DIM


def _hi_lo(x):
    hi = x.astype(BF16)
    return hi, (x - hi.astype(F32)).astype(BF16)


def _prep_a_kernel(qa_ref, ka_ref, va_ref, iq_ref, ik_ref, iw_ref, gq_ref, gk_ref, ta_ref, ti_ref,
                   qt_ref, ko_ref, vt_ref, iqt_ref, iko_ref, iwt_ref):
    half = HEAD_DIM // ROPE_FRACTION // 2
    q = jnp.concatenate(_rope(_head_rms(qa_ref[...], gq_ref[...]), ta_ref, half), axis=1) * HEAD_DIM ** -0.5
    qt_ref[...] = q.T.astype(BF16)
    k = _head_rms(ka_ref[...], gk_ref[...])
    ko_ref[...] = jnp.concatenate(_rope(k, ta_ref, half), axis=1).astype(BF16)
    vt_ref[...] = va_ref[...].T.astype(BF16)
    ihalf = IDX_DIM // ROPE_FRACTION // 2
    iq = jnp.concatenate(_rope(iq_ref[...], ti_ref, ihalf), axis=1) * IDX_DIM ** -0.5
    hi, lo = _hi_lo(iq.T)
    for h in range(IDX_HEADS):
        rows = slice(h * IDX_DIM, (h + 1) * IDX_DIM)
        for part, val in enumerate((hi, hi, lo, lo)):
            base = h * IDX_K + part * IDX_DIM
            iqt_ref[base:base + IDX_DIM, :] = val[rows, :]
    ik = _rope(ik_ref[...], ti_ref, ihalf)[0]
    ik_hi, ik_lo = _hi_lo(ik)
    ika = ik_hi.astype(F32) + pltpu.roll(ik_lo.astype(F32), IDX_DIM, 1)
    iko_ref[...] = jnp.concatenate([ika, ika], axis=1).astype(BF16)
    iwt_ref[...] = (iw_ref[...] * IDX_HEADS ** -0.5).T[0:IDX_HEADS, :]


def _prep_a(proj, gq, gk, tab_a, tab_i):
    bsz, s_len, _ = proj.shape
    ts = DSA_T
    n_t = s_len // ts

    def col(c, w):
        return pl.BlockSpec((None, ts, w), lambda s, b, c=c: (b, s, c))

    def rows_out(w):
        return pl.BlockSpec((None, ts, w), lambda s, b: (b, s, 0))

    def tile_out(r):
        return pl.BlockSpec((None, None, r, ts), lambda s, b: (b, s, 0, 0))

    tab_spec = pl.BlockSpec((3, ts, LANES), lambda s, b: (0, s, 0))
    g_spec = pl.BlockSpec((1, HEAD_DIM), lambda s, b: (0, 0))
    sd = jax.ShapeDtypeStruct
    return pl.pallas_call(
        _prep_a_kernel,
        grid=(n_t, bsz),
        in_specs=[col(COL_QA, W512), col(COL_KA, W512), col(COL_VA, W512), col(COL_IQ, W512),
                  col(COL128_IK, LANES), col(COL128_IW, LANES), g_spec, g_spec, tab_spec, tab_spec],
        out_specs=[tile_out(A_WIDTH), rows_out(A_WIDTH), tile_out(A_WIDTH), tile_out(IDX_HEADS * IDX_K),
                   rows_out(IDX_K), tile_out(IDX_HEADS)],
        out_shape=[sd((bsz, n_t, A_WIDTH, ts), BF16), sd((bsz, s_len, A_WIDTH), BF16),
                   sd((bsz, n_t, A_WIDTH, ts), BF16), sd((bsz, n_t, IDX_HEADS * IDX_K, ts), BF16),
                   sd((bsz, s_len, IDX_K), BF16), sd((bsz, n_t, IDX_HEADS, ts), F32)],
        compiler_params=_cparams(("parallel", "parallel")),
        name="prep_a",
    )(proj, proj, proj, proj, proj, proj, gq.reshape(1, HEAD_DIM), gk.reshape(1, HEAD_DIM), tab_a, tab_i)


def _rope_tables(s_len, period, reps):
    rot = period // ROPE_FRACTION
    half = rot // 2
    inv = jnp.power(ROPE_THETA, -jnp.arange(half, dtype=F32) * 2.0 / rot)
    ang = jnp.arange(s_len).astype(F32)[:, None] * inv[None, :]
    cos, sin = jnp.cos(ang), jnp.sin(ang)
    pad = jnp.zeros((s_len, period - rot), F32)
    zero = jnp.zeros((s_len, half), F32)
    c = jnp.concatenate([cos, cos, pad + 1.0], axis=1)
    s1 = jnp.concatenate([-sin, zero, pad], axis=1)
    s2 = jnp.concatenate([zero, sin, pad], axis=1)
    return jnp.stack([jnp.tile(t, (1, reps)) for t in (c, s1, s2)])


INT_MAX = np.int32(2 ** 31 - 1)
MASKED = -1e30
SUBLANES = 8


def _dsa_kernel(topk, qt_ref, iqt_ref, iwt_ref, k_ref, vt_ref, ik_ref, o_ref, key_ref, acc_ref, jb_ref):
    t = DSA_T
    qi = pl.program_id(1)
    n_kt = qi + 1
    s_len = k_ref.shape[0]
    qpos = lax.broadcasted_iota(jnp.int32, (1, t), 1)
    limit = qi * t + (qpos // CHUNK + 1) * CHUNK
    krow = lax.broadcasted_iota(jnp.int32, (t, t), 0)

    def score_tile(kt, carry):
        start = pl.multiple_of(kt * t, t)
        ikt = ik_ref[pl.ds(start, t), :]
        score = jnp.zeros((t, t), F32)
        for h in range(IDX_HEADS):
            logit = jnp.dot(ikt, iqt_ref[h * IDX_K:(h + 1) * IDX_K, :], preferred_element_type=F32)
            score = score + jnp.maximum(logit, 0.0) * iwt_ref[h:h + 1, :]
        bits = pltpu.bitcast(score, jnp.int32)
        key = bits ^ ((bits >> 31) & jnp.int32(0x7FFFFFFF))
        key_ref[kt] = jnp.where(krow + start < limit, key, INT_MIN)
        return carry

    lax.fori_loop(0, n_kt, score_tile, 0)

    def count(pred):
        def body(kt, acc):
            ones = jnp.where(pred(kt), 1.0, 0.0)
            return acc + jnp.sum(ones.reshape(t // SUBLANES, SUBLANES, t), axis=0)
        acc = lax.fori_loop(0, n_kt, body, jnp.zeros((SUBLANES, t), F32))
        return jnp.sum(acc, axis=0, keepdims=True)

    def vbody(it, carry):
        lo, n_lo = carry
        cand = lo + lax.shift_left(jnp.int32(1), 31 - it)
        n = count(lambda kt: key_ref[kt] >= cand)
        ok = n >= topk
        return jnp.where(ok, cand, lo), jnp.where(ok, n, n_lo)

    v, n_ge = lax.fori_loop(0, 32, vbody, (jnp.full((1, t), INT_MIN, jnp.int32),
                                           jnp.full((1, t), 1.0, F32) * (n_kt * t).astype(F32)))

    jb_ref[...] = jnp.full(jb_ref.shape, INT_MAX, jnp.int32)

    @pl.when(jnp.max(n_ge) > topk)
    def _():
        need = topk - count(lambda kt: key_ref[kt] > v)
        nbits = (s_len - 1).bit_length()

        def jbody(it, jb):
            cand = jb + lax.shift_left(jnp.int32(1), nbits - 1 - it)
            few = count(lambda kt: (key_ref[kt] == v) & (krow + kt * t < cand)) < need
            return jnp.where(few, cand, jb)

        jb = lax.fori_loop(0, nbits, jbody, jnp.zeros((1, t), jnp.int32))
        jb_ref[...] = jnp.broadcast_to(jb, jb_ref.shape)

    jb = jb_ref[0:1, :]

    acc_ref[...] = jnp.zeros_like(acc_ref)

    def att_tile(kt, carry):
        ms, ls = carry
        start = pl.multiple_of(kt * t, t)
        key = key_ref[kt]
        pos = krow + start
        sel = ((key > v) | ((key == v) & (pos <= jb))) & (pos < limit)
        heads = [slice(h * HEAD_DIM, (h + 1) * HEAD_DIM) for h in range(N_HEADS_A)]
        ss = [jnp.where(sel, jnp.dot(k_ref[pl.ds(start, t), hs], qt_ref[hs, :], preferred_element_type=F32),
                        MASKED) for hs in heads]
        new_ms = [jnp.maximum(ms[h], jnp.max(ss[h], axis=0, keepdims=True)) for h in range(N_HEADS_A)]
        ps = [jnp.exp(ss[h] - new_ms[h]) for h in range(N_HEADS_A)]
        alphas = [jnp.exp(ms[h] - new_ms[h]) for h in range(N_HEADS_A)]
        new_ls = [alphas[h] * ls[h] + jnp.sum(ps[h], axis=0, keepdims=True) for h in range(N_HEADS_A)]
        pvs = [jnp.dot(vt_ref[kt, hs, :], ps[h].astype(BF16), preferred_element_type=F32)
               for h, hs in enumerate(heads)]
        for h in range(N_HEADS_A):
            acc_ref[h] = alphas[h] * acc_ref[h] + pvs[h]
        return tuple(new_ms), tuple(new_ls)

    init = (tuple(jnp.full((1, t), MASKED, F32) for _ in range(N_HEADS_A)),
            tuple(jnp.zeros((1, t), F32) for _ in range(N_HEADS_A)))
    _, ls = lax.fori_loop(0, n_kt, att_tile, init)
    for h in range(N_HEADS_A):
        o_ref[:, h * HEAD_DIM:(h + 1) * HEAD_DIM] = (acc_ref[h] / ls[h]).T.astype(o_ref.dtype)


def _dsa(qt, ka, vt, iqt, ik_ext, iwt):
    bsz, s_len, _ = ka.shape
    t = DSA_T
    n_t = s_len // t
    topk = min(IDX_TOPK_MAX, s_len // 4)

    def qtile(r):
        return pl.BlockSpec((None, None, r, t), lambda b, i: (b, i, 0, 0))

    return pl.pallas_call(
        functools.partial(_dsa_kernel, topk),
        grid=(bsz, n_t),
        in_specs=[qtile(A_WIDTH), qtile(IDX_HEADS * IDX_K), qtile(IDX_HEADS),
                  pl.BlockSpec((None, s_len, A_WIDTH), lambda b, i: (b, 0, 0)),
                  pl.BlockSpec((None, n_t, A_WIDTH, t), lambda b, i: (b, 0, 0, 0)),
                  pl.BlockSpec((None, s_len, IDX_K), lambda b, i: (b, 0, 0))],
        out_specs=pl.BlockSpec((None, t, A_WIDTH), lambda b, i: (b, i, 0)),
        out_shape=jax.ShapeDtypeStruct((bsz, s_len, A_WIDTH), BF16),
        scratch_shapes=[pltpu.VMEM((n_t, t, t), jnp.int32),
                        pltpu.VMEM((N_HEADS_A, HEAD_DIM, t), F32),
                        pltpu.VMEM((SUBLANES, t), jnp.int32)],
        compiler_params=_cparams(("parallel", "arbitrary")),
        name="dsa",
    )(qt, iqt, iwt, ka, vt, ik_ext)


GLA_ROWS = 256


def _gla_kernel(q_ref, k_ref, v_ref, og_ref, gl_ref, wg_ref, bg_ref, gn_ref, o_ref, st_ref):
    @pl.when(pl.program_id(1) == 0)
    def _():
        st_ref[...] = jnp.zeros_like(st_ref)

    gate = jnp.dot(gl_ref[...], wg_ref[...], preferred_element_type=F32,
                   precision=lax.Precision.HIGHEST) + bg_ref[...]
    lg = _log_sigmoid(gate) * (1.0 / GLA_TAU)
    row = lax.broadcasted_iota(jnp.int32, (CHUNK, CHUNK), 0)
    colm = lax.broadcasted_iota(jnp.int32, (CHUNK, CHUNK), 1)
    causal = row >= colm
    tril = jnp.where(causal, 1.0, 0.0)
    mid = CHUNK // 2
    for ci in range(q_ref.shape[0] // CHUNK):
        rs = slice(ci * CHUNK, (ci + 1) * CHUNK)
        for h in range(N_HEADS_B):
            cs = slice(h * LANES, (h + 1) * LANES)
            b = jnp.dot(tril, lg[rs, cs], preferred_element_type=F32, precision=lax.Precision.HIGHEST)
            q = q_ref[rs, cs] * GLA_DK ** -0.5
            k = k_ref[rs, cs]
            v = v_ref[rs, cs].astype(BF16)
            st = st_ref[h]
            o = _dot_t((q * jnp.exp(b)).astype(BF16), st.astype(BF16))
            bm = b[mid:mid + 1, :]
            att = _dot_t((q * jnp.exp(b - bm)).astype(BF16), (k * jnp.exp(bm - b)).astype(BF16))
            att = jnp.where(causal, att, 0.0)
            o = o + jnp.dot(att.astype(BF16), v, preferred_element_type=F32)
            bl = b[CHUNK - 1:CHUNK, :]
            kd = (k * jnp.exp(bl - b)).astype(BF16)
            st_ref[h] = st * jnp.exp(bl) + lax.dot_general(
                v, kd, (((0,), (0,)), ((), ())), preferred_element_type=F32)
            og = og_ref[rs, cs]
            o_ref[rs, cs] = (_rms(o, gn_ref[...]) * (og * jax.nn.sigmoid(og))).astype(o_ref.dtype)


def _gla(proj, wg, bg, gn):
    bsz, s_len, _ = proj.shape

    def col(c, w):
        return pl.BlockSpec((None, GLA_ROWS, w), lambda b, s, c=c: (b, s, c))

    return pl.pallas_call(
        _gla_kernel,
        grid=(bsz, s_len // GLA_ROWS),
        in_specs=[col(COL_QB, W512), col(COL_KB, W512), col(COL_VB, W512), col(COL_OG, W512),
                  col(COL128_GLOW, LANES),
                  pl.BlockSpec((LANES, W512), lambda b, s: (0, 0)),
                  pl.BlockSpec((1, W512), lambda b, s: (0, 0)),
                  pl.BlockSpec((1, GLA_DV), lambda b, s: (0, 0))],
        out_specs=pl.BlockSpec((None, GLA_ROWS, B_WIDTH), lambda b, s: (b, s, 0)),
        out_shape=jax.ShapeDtypeStruct((bsz, s_len, B_WIDTH), BF16),
        scratch_shapes=[pltpu.VMEM((N_HEADS_B, GLA_DV, LANES), F32)],
        compiler_params=_cparams(("parallel", "arbitrary")),
        name="gla",
    )(proj, proj, proj, proj, proj, wg, bg, gn.reshape(1, GLA_DV))


SB_T = 256
SB_HEADS = 2
SB_W = SB_HEADS * HEAD_DIM
LOG2E = 1.4426950408889634


def _sb_kernel(q_ref, k_ref, v_ref, gq_ref, gk_ref, cw_ref, o_ref, kn_ref, vb_ref, acc_ref, car_ref, zn_ref):
    i = pl.program_id(2)
    heads = [slice(h * HEAD_DIM, (h + 1) * HEAD_DIM) for h in range(SB_HEADS)]

    @pl.when(i == 0)
    def _():
        for hs in heads:
            kn_ref[:, hs] = _rms(k_ref[:, hs], gk_ref[...]).astype(BF16)
        vb_ref[...] = v_ref[...].astype(BF16)

    qs = [(_rms(q_ref[:, hs], gq_ref[...]) * (HEAD_DIM ** -0.5 * LOG2E)).astype(BF16) for hs in heads]

    def logits(j):
        start = pl.multiple_of(j * SB_T, SB_T)
        return [_dot_t(qs[h], kn_ref[pl.ds(start, SB_T), hs]) for h, hs in enumerate(heads)]

    def block(j, zs, next_slot, diag):
        start = pl.multiple_of(j * SB_T, SB_T)
        if diag:
            strict = (lax.broadcasted_iota(jnp.int32, (SB_T, SB_T), 1)
                      < lax.broadcasted_iota(jnp.int32, (SB_T, SB_T), 0))
        for h, z in enumerate(logits(jnp.maximum(j - 1, 0))):
            zn_ref[next_slot * SB_HEADS + h] = z
        hls = []
        for z in zs:
            sp = jnp.maximum(z, 0.0) + jnp.log2(1.0 + jnp.exp2(-jnp.abs(z)))
            if diag:
                sp = jnp.where(strict, sp, 0.0)
            sp_hi = sp.astype(BF16)
            sp_lo = (sp - sp_hi.astype(F32)).astype(BF16)
            hls.append(jnp.concatenate([sp_hi, sp_lo], axis=1))
        rs = [jnp.dot(hl, cw_ref[...], preferred_element_type=F32) for hl in hls]
        avs = []
        for h in range(SB_HEADS):
            car = car_ref[h]
            a = jnp.exp2(zs[h] - rs[h] - jnp.concatenate([car] * (SB_T // LANES), axis=1))
            if diag:
                a = jnp.where(strict, a, 0.0)
            avs.append(a.astype(BF16))
            car_ref[h] = car + jnp.broadcast_to(rs[h][:, 0:1], car.shape)
        for h, hs in enumerate(heads):
            acc_ref[:, hs] += jnp.dot(avs[h], vb_ref[pl.ds(start, SB_T), hs], preferred_element_type=F32)

    acc_ref[...] = jnp.zeros_like(acc_ref)
    car_ref[...] = jnp.zeros_like(car_ref)
    block(i, logits(i), 0, True)

    def body(t, carry):
        slot = t % 2
        zs = [zn_ref[slot * SB_HEADS + h] for h in range(SB_HEADS)]
        block(i - 1 - t, zs, 1 - slot, False)
        return carry

    lax.fori_loop(0, i, body, 0)
    o_ref[...] = acc_ref[...].astype(o_ref.dtype)


def _stick_breaking(qkv, gq, gk):
    bsz, s_len, _ = qkv.shape
    g_spec = pl.BlockSpec((1, HEAD_DIM), lambda b, h, i: (0, 0))
    tri = (jnp.arange(SB_T)[:, None] >= jnp.arange(SB_T)[None, :]).astype(BF16)
    cum_w = jnp.concatenate([tri, tri], axis=0)
    n_grp = N_HEADS_C // SB_HEADS
    return pl.pallas_call(
        _sb_kernel,
        grid=(bsz, n_grp, s_len // SB_T),
        in_specs=[pl.BlockSpec((None, SB_T, SB_W), lambda b, h, i: (b, i, h)),
                  pl.BlockSpec((None, s_len, SB_W), lambda b, h, i: (b, 0, n_grp + h)),
                  pl.BlockSpec((None, s_len, SB_W), lambda b, h, i: (b, 0, 2 * n_grp + h)),
                  g_spec, g_spec,
                  pl.BlockSpec((2 * SB_T, SB_T), lambda b, h, i: (0, 0))],
        out_specs=pl.BlockSpec((None, SB_T, SB_W), lambda b, h, i: (b, i, h)),
        out_shape=jax.ShapeDtypeStruct((bsz, s_len, C_WIDTH), BF16),
        scratch_shapes=[pltpu.VMEM((s_len, SB_W), BF16), pltpu.VMEM((s_len, SB_W), BF16),
                        pltpu.VMEM((SB_T, SB_W), F32), pltpu.VMEM((SB_HEADS, SB_T, LANES), F32),
                        pltpu.VMEM((2 * SB_HEADS, SB_T, SB_T), F32)],
        compiler_params=_cparams(("parallel", "parallel", "arbitrary")),
        name="stick_breaking",
    )(qkv, qkv, qkv, gq.reshape(1, HEAD_DIM), gk.reshape(1, HEAD_DIM), cum_w)


def _pad_heads(w, n_heads, width):
    lead = w.shape[:-1]
    d = w.shape[-1] // n_heads
    w = w.reshape(*lead, n_heads, d)
    w = jnp.pad(w, [(0, 0)] * len(lead) + [(0, 0), (0, width - d)])
    return w.reshape(*lead, n_heads * width)


def _pad_cols(w, width):
    return jnp.pad(w, [(0, 0)] * (w.ndim - 1) + [(0, width - w.shape[-1])])


def _layer0_weights(w_in):
    parts, o = [], 0
    for s in AB_SIZES:
        parts.append(w_in[:, o:o + s])
        o += s
    qa, ka, va, iq, ik, iw, qb, kb, vb, glow, og = parts
    cols = [qa, ka, va, iq, _pad_heads(qb, N_HEADS_B, LANES), _pad_heads(kb, N_HEADS_B, LANES), vb, og,
            _pad_cols(ik, LANES), _pad_cols(iw, LANES), _pad_cols(glow, LANES)]
    return jnp.concatenate(cols, axis=1).astype(BF16)


def kernel(x, g_mix, g_ffn, w_in_ab, gq_a, gk_a, w_gate_up, b_gate, g_gla, w_out_ab,
           w_in_c, gq_c, gk_c, w_out_c, w_up, w_down):
    bsz, s_len, d = x.shape
    m = bsz * s_len
    h = x.reshape(m, d)

    proj = _norm_matmul(h, g_mix[0], _layer0_weights(w_in_ab[0]), tm=1024, tn=896)
    proj = proj.reshape(bsz, s_len, AB_PAD)
    tab_a = _rope_tables(s_len, HEAD_DIM, 1)
    tab_i = _rope_tables(s_len, IDX_DIM, LANES // IDX_DIM)
    oa = _dsa(*_prep_a(proj, gq_a[0], gk_a[0], tab_a, tab_i))
    wg = jnp.pad(_pad_heads(w_gate_up[0], N_HEADS_B, LANES), ((0, LANES - GLA_GATE_RANK), (0, 0)))
    bg = _pad_heads(b_gate[0], N_HEADS_B, LANES).reshape(1, W512)
    ob = _gla(proj, wg, bg, g_gla[0])
    w_o = w_out_ab[0].astype(BF16)
    h = _proj_mlp(h, [oa.reshape(m, A_WIDTH), ob.reshape(m, B_WIDTH)], [w_o[:A_WIDTH], w_o[A_WIDTH:]],
                  g_ffn[0], w_up[0].astype(BF16), w_down[0].astype(BF16), tm=512, tf=512)

    qkv = _norm_matmul(h, g_mix[1], w_in_c[0].astype(BF16), tm=1024, tn=1024)
    oc = _stick_breaking(qkv.reshape(bsz, s_len, 3 * C_WIDTH), gq_c[0], gk_c[0])
    h = _proj_mlp(h, [oc.reshape(m, C_WIDTH)], [w_out_c[0].astype(BF16)],
                  g_ffn[1], w_up[1].astype(BF16), w_down[1].astype(BF16), tm=512, tf=512)
    return h.reshape(bsz, s_len, d)
```

```python
import functools

import numpy as np
import jax
import jax.numpy as jnp
from jax import lax
from jax.experimental import pallas as pl
from jax.experimental.pallas import tpu as pltpu

F32 = jnp.float32
BF16 = jnp.bfloat16

D_MODEL = 1024
CHUNK = 64
HEAD_DIM = 128
EPS = 1e-6
N_HEADS_A = 4
A_WIDTH = N_HEADS_A * HEAD_DIM
IDX_HEADS = 8
IDX_DIM = 64
IDX_TOPK_MAX = 256
N_HEADS_B = 4
GLA_DK = 64
GLA_DV = 128
GLA_GATE_RANK = 16
GLA_TAU = 16.0
B_WIDTH = N_HEADS_B * GLA_DV
N_HEADS_C = 8
C_WIDTH = N_HEADS_C * HEAD_DIM
ROPE_THETA = 500000.0
ROPE_FRACTION = 4
D_FF = 4 * D_MODEL
AB_SIZES = (A_WIDTH, A_WIDTH, A_WIDTH, IDX_HEADS * IDX_DIM, IDX_DIM, IDX_HEADS,
            N_HEADS_B * GLA_DK, N_HEADS_B * GLA_DK, B_WIDTH, GLA_GATE_RANK, B_WIDTH)

LANES = 128
VMEM_LIMIT = 48 * 1024 * 1024
MLP_TM, MLP_TF = 1024, 512
PROJ_TM = 1024

W512 = 512
P16_QA, P16_KA, P16_VA, P16_QB, P16_KB, P16_VB, P16_OG = range(7)
N_P16 = 7
P32_IQ = 0
P32_IK, P32_IW, P32_GLOW = 4, 5, 6
N_P32 = 2

INT_MIN = np.int32(-2 ** 31)


def _cparams(sem):
    return pltpu.CompilerParams(dimension_semantics=sem, vmem_limit_bytes=VMEM_LIMIT)


def _rms(x, g):
    return x * lax.rsqrt(jnp.mean(x * x, axis=-1, keepdims=True) + EPS) * g


def _log_sigmoid(x):
    return jnp.minimum(x, 0.0) - jnp.log1p(jnp.exp(-jnp.abs(x)))


def _dot_t(a, b):
    return lax.dot_general(a, b, (((1,), (1,)), ((), ())), preferred_element_type=F32)


def _norm_matmul_kernel(n16, x_ref, g_ref, w_ref, *refs):
    o16_ref, xn_ref = refs[0], refs[-1]
    j = pl.program_id(1)

    @pl.when(j == 0)
    def _():
        xn_ref[...] = _rms(x_ref[...], g_ref[...]).astype(BF16)

    y = jnp.dot(xn_ref[...], w_ref[...], preferred_element_type=F32)
    if len(refs) == 2:
        o16_ref[...] = y.astype(BF16)
    else:
        @pl.when(j < n16)
        def _():
            o16_ref[...] = y.astype(BF16)

        @pl.when(j >= n16)
        def _():
            refs[1][...] = y


def _norm_matmul(x, g, w, tm, tn, n16=None):
    m, d = x.shape
    n_t = w.shape[1] // tn
    split = n16 is not None
    n16 = n16 if split else n_t
    out_specs = [pl.BlockSpec((tm, tn), lambda i, j: (i, jnp.minimum(j, n16 - 1)))]
    out_shape = [jax.ShapeDtypeStruct((m, n16 * tn), BF16)]
    if split:
        out_specs.append(pl.BlockSpec((tm, tn), lambda i, j: (i, jnp.maximum(j - n16, 0))))
        out_shape.append(jax.ShapeDtypeStruct((m, (n_t - n16) * tn), F32))
    out = pl.pallas_call(
        functools.partial(_norm_matmul_kernel, n16),
        grid=(m // tm, n_t),
        in_specs=[pl.BlockSpec((tm, d), lambda i, j: (i, 0)),
                  pl.BlockSpec((1, d), lambda i, j: (0, 0)),
                  pl.BlockSpec((d, tn), lambda i, j: (0, j))],
        out_specs=out_specs,
        out_shape=out_shape,
        scratch_shapes=[pltpu.VMEM((tm, d), BF16)],
        compiler_params=_cparams(("parallel", "arbitrary")),
        name="norm_matmul",
    )(x, g.reshape(1, d), w)
    return tuple(out) if split else out[0]


def _proj_mlp_kernel(n_a, h_ref, *refs):
    a_refs = refs[:n_a]
    wo_refs = refs[n_a:2 * n_a]
    g_ref, wu_ref, wd_ref, o_ref, acc_ref, xn_ref = refs[2 * n_a:]
    f = pl.program_id(1)

    @pl.when(f == 0)
    def _():
        h1 = h_ref[...]
        for a_ref, wo_ref in zip(a_refs, wo_refs):
            h1 = h1 + jnp.dot(a_ref[...], wo_ref[...], preferred_element_type=F32)
        acc_ref[...] = h1
        xn_ref[...] = _rms(h1, g_ref[...]).astype(BF16)

    u = jnp.dot(xn_ref[...], wu_ref[...], preferred_element_type=F32)
    u = jnp.square(jnp.maximum(u, 0.0)).astype(BF16)
    acc_ref[...] += jnp.dot(u, wd_ref[...], preferred_element_type=F32)

    @pl.when(f == pl.num_programs(1) - 1)
    def _():
        o_ref[...] = acc_ref[...]


def _proj_mlp(h, a_list, wo_list, g, w_up, w_down, tm, tf):
    m, d = h.shape
    ff = w_up.shape[1]
    n_a = len(a_list)
    in_specs = [pl.BlockSpec((tm, d), lambda i, f: (i, 0))]
    in_specs += [pl.BlockSpec((tm, a.shape[1]), lambda i, f: (i, 0)) for a in a_list]
    in_specs += [pl.BlockSpec(w.shape, lambda i, f: (0, 0)) for w in wo_list]
    in_specs += [pl.BlockSpec((1, d), lambda i, f: (0, 0)),
                 pl.BlockSpec((d, tf), lambda i, f: (0, f)),
                 pl.BlockSpec((tf, d), lambda i, f: (f, 0))]
    return pl.pallas_call(
        functools.partial(_proj_mlp_kernel, n_a),
        grid=(m // tm, ff // tf),
        in_specs=in_specs,
        out_specs=pl.BlockSpec((tm, d), lambda i, f: (i, 0)),
        out_shape=jax.ShapeDtypeStruct((m, d), F32),
        scratch_shapes=[pltpu.VMEM((tm, d), F32), pltpu.VMEM((tm, d), BF16)],
        compiler_params=_cparams(("parallel", "arbitrary")),
        name="proj_mlp",
    )(h, *a_list, *wo_list, g.reshape(1, d), w_up, w_down)


def _rope(x, tab_ref, half):
    out = []
    for s in range(x.shape[1] // LANES):
        xs = x[:, s * LANES:(s + 1) * LANES]
        out.append(xs * tab_ref[0]
                   + pltpu.roll(xs, LANES - half, 1) * tab_ref[1]
                   + pltpu.roll(xs, half, 1) * tab_ref[2])
    return out


def _head_rms(x, g):
    out = []
    for s in range(x.shape[1] // HEAD_DIM):
        out.append(_rms(x[:, s * HEAD_DIM:(s + 1) * HEAD_DIM], g))
    return jnp.concatenate(out, axis=1)


DSA_T = 256
IDX_K = 4 * IDX_DIM


def _hi_lo(x):
    hi = x.astype(BF16)
    return hi, (x - hi.astype(F32)).astype(BF16)


def _prep_a_kernel(qa_ref, ka_ref, va_ref, iq_ref, ik_ref, iw_ref, gq_ref, gk_ref, ta_ref, ti_ref,
                   qt_ref, ko_ref, vt_ref, iqt_ref, iko_ref, iwt_ref):
    half = HEAD_DIM // ROPE_FRACTION // 2
    q = _head_rms(qa_ref[...].astype(F32), gq_ref[...])
    q = jnp.concatenate(_rope(q, ta_ref, half), axis=1) * HEAD_DIM ** -0.5
    qt_ref[...] = q.T.astype(BF16)
    k = _head_rms(ka_ref[...].astype(F32), gk_ref[...])
    ko_ref[...] = jnp.concatenate(_rope(k, ta_ref, half), axis=1).astype(BF16)
    vt_ref[...] = va_ref[...].astype(F32).T.astype(BF16)
    ihalf = IDX_DIM // ROPE_FRACTION // 2
    iq = jnp.concatenate(_rope(iq_ref[...], ti_ref, ihalf), axis=1) * IDX_DIM ** -0.5
    hi, lo = _hi_lo(iq.T)
    for h in range(IDX_HEADS):
        rows = slice(h * IDX_DIM, (h + 1) * IDX_DIM)
        for part, val in enumerate((hi, hi, lo, lo)):
            base = h * IDX_K + part * IDX_DIM
            iqt_ref[base:base + IDX_DIM, :] = val[rows, :]
    ik = _rope(ik_ref[...], ti_ref, ihalf)[0]
    ik_hi, ik_lo = _hi_lo(ik)
    ika = ik_hi.astype(F32) + pltpu.roll(ik_lo.astype(F32), IDX_DIM, 1)
    iko_ref[...] = jnp.concatenate([ika, ika], axis=1).astype(BF16)
    iwt_ref[...] = (iw_ref[...] * IDX_HEADS ** -0.5).T[0:IDX_HEADS, :]


def _prep_a(p16, p32, gq, gk, tab_a, tab_i):
    bsz, s_len, _ = p16.shape
    ts = DSA_T
    n_t = s_len // ts

    def col(c, w):
        return pl.BlockSpec((None, ts, w), lambda s, b, c=c: (b, s, c))

    def rows_out(w):
        return pl.BlockSpec((None, ts, w), lambda s, b: (b, s, 0))

    def tile_out(r):
        return pl.BlockSpec((None, None, r, ts), lambda s, b: (b, s, 0, 0))

    tab_spec = pl.BlockSpec((3, ts, LANES), lambda s, b: (0, s, 0))
    g_spec = pl.BlockSpec((1, HEAD_DIM), lambda s, b: (0, 0))
    sd = jax.ShapeDtypeStruct
    return pl.pallas_call(
        _prep_a_kernel,
        grid=(n_t, bsz),
        in_specs=[col(P16_QA, W512), col(P16_KA, W512), col(P16_VA, W512), col(P32_IQ, W512),
                  col(P32_IK, LANES), col(P32_IW, LANES), g_spec, g_spec, tab_spec, tab_spec],
        out_specs=[tile_out(A_WIDTH), rows_out(A_WIDTH), tile_out(A_WIDTH), tile_out(IDX_HEADS * IDX_K),
                   rows_out(IDX_K), tile_out(IDX_HEADS)],
        out_shape=[sd((bsz, n_t, A_WIDTH, ts), BF16), sd((bsz, s_len, A_WIDTH), BF16),
                   sd((bsz, n_t, A_WIDTH, ts), BF16), sd((bsz, n_t, IDX_HEADS * IDX_K, ts), BF16),
                   sd((bsz, s_len, IDX_K), BF16), sd((bsz, n_t, IDX_HEADS, ts), F32)],
        compiler_params=_cparams(("parallel", "parallel")),
        name="prep_a",
    )(p16, p16, p16, p32, p32, p32, gq.reshape(1, HEAD_DIM), gk.reshape(1, HEAD_DIM), tab_a, tab_i)


def _rope_tables(s_len, period, reps):
    rot = period // ROPE_FRACTION
    half = rot // 2
    inv = jnp.power(ROPE_THETA, -jnp.arange(half, dtype=F32) * 2.0 / rot)
    ang = jnp.arange(s_len).astype(F32)[:, None] * inv[None, :]
    cos, sin = jnp.cos(ang), jnp.sin(ang)
    pad = jnp.zeros((s_len, period - rot), F32)
    zero = jnp.zeros((s_len, half), F32)
    c = jnp.concatenate([cos, cos, pad + 1.0], axis=1)
    s1 = jnp.concatenate([-sin, zero, pad], axis=1)
    s2 = jnp.concatenate([zero, sin, pad], axis=1)
    return jnp.stack([jnp.tile(t, (1, reps)) for t in (c, s1, s2)])


INT_MAX = np.int32(2 ** 31 - 1)
MASKED = -1e30
SUBLANES = 8


def _dsa_kernel(topk, qt_ref, iqt_ref, iwt_ref, k_ref, vt_ref, ik_ref, o_ref,
                key_ref, acc_ref, jb_ref, v_ref, nge_ref):
    t = DSA_T
    qi = pl.program_id(1)
    n_kt = qi + 1
    s_len = k_ref.shape[0]
    qpos = lax.broadcasted_iota(jnp.int32, (1, t), 1)
    limit = qi * t + (qpos // CHUNK + 1) * CHUNK
    krow = lax.broadcasted_iota(jnp.int32, (t, t), 0)

    def score_tile(kt, carry):
        start = pl.multiple_of(kt * t, t)
        ikt = ik_ref[pl.ds(start, t), :]
        score = jnp.zeros((t, t), F32)
        for h in range(IDX_HEADS):
            logit = jnp.dot(ikt, iqt_ref[h * IDX_K:(h + 1) * IDX_K, :], preferred_element_type=F32)
            score = score + jnp.maximum(logit, 0.0) * iwt_ref[h:h + 1, :]
        bits = pltpu.bitcast(score, jnp.int32)
        key = bits ^ ((bits >> 31) & jnp.int32(0x7FFFFFFF))
        key_ref[kt] = jnp.where(krow + start < limit, key, INT_MIN)
        return carry

    lax.fori_loop(0, n_kt, score_tile, 0)

    def count(pred, n=None):
        def body(kt, acc):
            ones = jnp.where(pred(kt), 1.0, 0.0)
            return acc + jnp.sum(ones.reshape(t // SUBLANES, SUBLANES, t), axis=0)
        acc = jnp.zeros((SUBLANES, t), F32)
        if n is None:
            acc = lax.fori_loop(0, n_kt, body, acc)
        else:
            for kt in range(n):
                acc = body(kt, acc)
        return jnp.sum(acc, axis=0, keepdims=True)

    for n in range(1, key_ref.shape[0] + 1):
        @pl.when(n_kt == n)
        def _(n=n):
            def vbody(it, carry):
                lo, n_lo = carry
                cand = lo + lax.shift_left(jnp.int32(1), 31 - it)
                cnt = count(lambda kt: key_ref[kt] >= cand, n)
                ok = cnt >= topk
                return jnp.where(ok, cand, lo), jnp.where(ok, cnt, n_lo)

            lo, n_lo = lax.fori_loop(0, 32, vbody, (jnp.full((1, t), INT_MIN, jnp.int32),
                                                    jnp.full((1, t), float(n * t), F32)))
            v_ref[...] = jnp.broadcast_to(lo, v_ref.shape)
            nge_ref[...] = jnp.broadcast_to(n_lo, nge_ref.shape)

    v = v_ref[0:1, :]
    n_ge = nge_ref[0:1, :]

    jb_ref[...] = jnp.full(jb_ref.shape, INT_MAX, jnp.int32)

    @pl.when(jnp.max(n_ge) > topk)
    def _():
        need = topk - count(lambda kt: key_ref[kt] > v)
        nbits = (s_len - 1).bit_length()

        def jbody(it, jb):
            cand = jb + lax.shift_left(jnp.int32(1), nbits - 1 - it)
            few = count(lambda kt: (key_ref[kt] == v) & (krow + kt * t < cand)) < need
            return jnp.where(few, cand, jb)

        jb = lax.fori_loop(0, nbits, jbody, jnp.zeros((1, t), jnp.int32))
        jb_ref[...] = jnp.broadcast_to(jb, jb_ref.shape)

    jb = jnp.where(v == INT_MIN, -1, jb_ref[0:1, :])

    acc_ref[...] = jnp.zeros_like(acc_ref)

    def att_tile(kt, carry):
        ms, ls = carry
        start = pl.multiple_of(kt * t, t)
        key = key_ref[kt]
        tie_bias = jnp.where(key == v, jnp.where(krow + start <= jb, 0.0, MASKED), MASKED)
        bias = jnp.where(key > v, 0.0, tie_bias)
        heads = [slice(h * HEAD_DIM, (h + 1) * HEAD_DIM) for h in range(N_HEADS_A)]
        ss = [jnp.dot(k_ref[pl.ds(start, t), hs], qt_ref[hs, :], preferred_element_type=F32) + bias
              for hs in heads]
        new_ms = [jnp.maximum(ms[h], jnp.max(ss[h], axis=0, keepdims=True)) for h in range(N_HEADS_A)]
        ps = [jnp.exp(ss[h] - new_ms[h]) for h in range(N_HEADS_A)]
        alphas = [jnp.exp(ms[h] - new_ms[h]) for h in range(N_HEADS_A)]
        new_ls = [alphas[h] * ls[h] + jnp.sum(ps[h], axis=0, keepdims=True) for h in range(N_HEADS_A)]
        pvs = [jnp.dot(vt_ref[kt, hs, :], ps[h].astype(BF16), preferred_element_type=F32)
               for h, hs in enumerate(heads)]
        for h in range(N_HEADS_A):
            acc_ref[h] = alphas[h] * acc_ref[h] + pvs[h]
        return tuple(new_ms), tuple(new_ls)

    init = (tuple(jnp.full((1, t), MASKED, F32) for _ in range(N_HEADS_A)),
            tuple(jnp.zeros((1, t), F32) for _ in range(N_HEADS_A)))
    _, ls = lax.fori_loop(0, n_kt, att_tile, init)
    for h in range(N_HEADS_A):
        o_ref[:, h * HEAD_DIM:(h + 1) * HEAD_DIM] = (acc_ref[h] / ls[h]).T.astype(o_ref.dtype)


def _dsa(qt, ka, vt, iqt, ik_ext, iwt):
    bsz, s_len, _ = ka.shape
    t = DSA_T
    n_t = s_len // t
    topk = min(IDX_TOPK_MAX, s_len // 4)

    def qtile(r):
        return pl.BlockSpec((None, None, r, t), lambda b, i: (b, i, 0, 0))

    return pl.pallas_call(
        functools.partial(_dsa_kernel, topk),
        grid=(bsz, n_t),
        in_specs=[qtile(A_WIDTH), qtile(IDX_HEADS * IDX_K), qtile(IDX_HEADS),
                  pl.BlockSpec((None, s_len, A_WIDTH), lambda b, i: (b, 0, 0)),
                  pl.BlockSpec((None, n_t, A_WIDTH, t), lambda b, i: (b, 0, 0, 0)),
                  pl.BlockSpec((None, s_len, IDX_K), lambda b, i: (b, 0, 0))],
        out_specs=pl.BlockSpec((None, t, A_WIDTH), lambda b, i: (b, i, 0)),
        out_shape=jax.ShapeDtypeStruct((bsz, s_len, A_WIDTH), BF16),
        scratch_shapes=[pltpu.VMEM((n_t, t, t), jnp.int32),
                        pltpu.VMEM((N_HEADS_A, HEAD_DIM, t), F32),
                        pltpu.VMEM((SUBLANES, t), jnp.int32),
                        pltpu.VMEM((SUBLANES, t), jnp.int32),
                        pltpu.VMEM((SUBLANES, t), F32)],
        compiler_params=_cparams(("parallel", "arbitrary")),
        name="dsa",
    )(qt, iqt, iwt, ka, vt, ik_ext)


GLA_ROWS = 256


GLA_BLK = SUBLANES
GLA_NBLK = CHUNK // GLA_BLK
GLA_PAIRS = [(i, j) for i in range(GLA_NBLK) for j in range(i)]


def _split3(x):
    hi = x.astype(BF16)
    r1 = x - hi.astype(F32)
    mid = r1.astype(BF16)
    return hi, mid, (r1 - mid.astype(F32)).astype(BF16)


def _gla_cumsum_matrix():
    t = np.arange(CHUNK)
    blk = t // GLA_BLK
    incl = t[None, :] <= t[:, None]
    start = t[None, :] < (GLA_BLK * blk)[:, None]
    end = t[None, :] < (GLA_BLK * (blk + 1))[:, None]
    bounds = t[None, :] < (GLA_BLK * np.arange(GLA_NBLK))[:, None]
    pad = np.zeros((GLA_BLK, CHUNK), bool)
    m = np.concatenate([incl, start, end, bounds, pad], axis=0).astype(np.float32)
    return jnp.asarray(np.concatenate([m, m, m], axis=1), BF16)


def _gla_kernel(q_ref, k_ref, v_ref, og_ref, gl_ref, wg_ref, bg_ref, gn_ref, cm_ref, o_ref,
                st_ref, lg_ref, b_ref, kc_ref):
    @pl.when(pl.program_id(1) == 0)
    def _():
        st_ref[...] = jnp.zeros_like(st_ref)

    g_hi, g_mid, g_lo = (p.astype(F32) for p in _split3(gl_ref[...]))
    r = GLA_GATE_RANK
    packed = (g_hi + pltpu.roll(g_hi, r, 1) + pltpu.roll(g_mid, 2 * r, 1) + pltpu.roll(g_hi, 3 * r, 1)
              + pltpu.roll(g_lo, 4 * r, 1) + pltpu.roll(g_mid, 5 * r, 1))
    gate = jnp.dot(packed.astype(BF16), wg_ref[...], preferred_element_type=F32) + bg_ref[...]
    lg_ref[...] = _log_sigmoid(gate) * (1.0 / GLA_TAU)

    sub = lax.broadcasted_iota(jnp.int32, (GLA_BLK, LANES), 0)
    lane_c = lax.broadcasted_iota(jnp.int32, (GLA_BLK, CHUNK), 1)
    heads = [slice(h * LANES, (h + 1) * LANES) for h in range(N_HEADS_B)]

    def chunk(ci, carry):
        r0 = pl.multiple_of(ci * CHUNK, CHUNK)
        rows = pl.ds(r0, CHUNK)
        cums = []
        for hs in heads:
            parts = _split3(lg_ref[rows, hs])
            cums.append(jnp.dot(cm_ref[...], jnp.concatenate(parts, axis=0), preferred_element_type=F32))
        for h, hs in enumerate(heads):
            b_ref[:, hs] = cums[h][0:CHUNK]
        kc_ref[...] = k_ref[rows, :].astype(F32)
        atts, qes, kds, bls = [], [], [], []
        for h, hs in enumerate(heads):
            b = cums[h][0:CHUNK]
            b_start = cums[h][CHUNK:2 * CHUNK]
            b_end = cums[h][2 * CHUNK:3 * CHUNK]
            bound = cums[h][3 * CHUNK:3 * CHUNK + GLA_NBLK]
            q = q_ref[rows, hs].astype(F32) * GLA_DK ** -0.5
            k = kc_ref[:, hs]
            qb = q * jnp.exp(b - b_start)
            kb = (k * jnp.exp(b_end - b)).astype(BF16)
            dec = [jnp.exp(jnp.where(sub > j, bound - bound[j + 1:j + 2], -jnp.inf))
                   for j in range(GLA_NBLK - 1)]
            lhs = jnp.concatenate([qb[i * GLA_BLK:(i + 1) * GLA_BLK] * dec[j][i:i + 1]
                                   for i, j in GLA_PAIRS], axis=0).astype(BF16)
            cross = _dot_t(lhs, kb)
            att_rows = []
            for i in range(GLA_NBLK):
                att = jnp.zeros((GLA_BLK, CHUNK), F32)
                for g, (gi, j) in enumerate(GLA_PAIRS):
                    if gi == i:
                        att = jnp.where(lane_c // GLA_BLK == j, cross[g * GLA_BLK:(g + 1) * GLA_BLK], att)
                blk = slice(i * GLA_BLK, (i + 1) * GLA_BLK)
                qi, bi = q[blk], b[blk]
                for sl in range(GLA_BLK):
                    srow = pl.ds(i * GLA_BLK + sl, 1)
                    ks = jnp.broadcast_to(kc_ref[srow, hs], (GLA_BLK, LANES))
                    bs = jnp.broadcast_to(b_ref[srow, hs], (GLA_BLK, LANES))
                    e = jnp.exp(jnp.where(sub >= sl, bi - bs, -jnp.inf))
                    col = jnp.sum(qi * ks * e, axis=1, keepdims=True)
                    att = jnp.where(lane_c == i * GLA_BLK + sl, col, att)
                att_rows.append(att)
            atts.append(jnp.concatenate(att_rows, axis=0).astype(BF16))
            bl = b[CHUNK - 1:CHUNK]
            qes.append((q * jnp.exp(b)).astype(BF16))
            kds.append((k * jnp.exp(bl - b)).astype(BF16))
            bls.append(bl)
        for h, hs in enumerate(heads):
            v = v_ref[rows, hs]
            st = st_ref[h]
            o = _dot_t(qes[h], st.astype(BF16)) + jnp.dot(atts[h], v, preferred_element_type=F32)
            st_ref[h] = st * jnp.exp(bls[h]) + lax.dot_general(
                v, kds[h], (((0,), (0,)), ((), ())), preferred_element_type=F32)
            og = og_ref[rows, hs].astype(F32)
            o_ref[rows, hs] = (_rms(o, gn_ref[...]) * (og * jax.nn.sigmoid(og))).astype(o_ref.dtype)
        return carry

    lax.fori_loop(0, q_ref.shape[0] // CHUNK, chunk, 0)


def _gla(p16, p32, wg, bg, gn):
    bsz, s_len, _ = p16.shape

    def col(c, w):
        return pl.BlockSpec((None, GLA_ROWS, w), lambda b, s, c=c: (b, s, c))

    cm = _gla_cumsum_matrix()
    return pl.pallas_call(
        _gla_kernel,
        grid=(bsz, s_len // GLA_ROWS),
        in_specs=[col(P16_QB, W512), col(P16_KB, W512), col(P16_VB, W512), col(P16_OG, W512),
                  col(P32_GLOW, LANES),
                  pl.BlockSpec((LANES, W512), lambda b, s: (0, 0)),
                  pl.BlockSpec((1, W512), lambda b, s: (0, 0)),
                  pl.BlockSpec((1, GLA_DV), lambda b, s: (0, 0)),
                  pl.BlockSpec(cm.shape, lambda b, s: (0, 0))],
        out_specs=pl.BlockSpec((None, GLA_ROWS, B_WIDTH), lambda b, s: (b, s, 0)),
        out_shape=jax.ShapeDtypeStruct((bsz, s_len, B_WIDTH), BF16),
        scratch_shapes=[pltpu.VMEM((N_HEADS_B, GLA_DV, LANES), F32),
                        pltpu.VMEM((GLA_ROWS, W512), F32),
                        pltpu.VMEM((CHUNK, W512), F32),
                        pltpu.VMEM((CHUNK, W512), F32)],
        compiler_params=_cparams(("parallel", "arbitrary")),
        name="gla",
    )(p16, p16, p16, p16, p32, wg, bg, gn.reshape(1, GLA_DV), cm)


def _gla_gate_weights(w_gate_up):
    hi, mid, lo = _split3(_pad_heads(w_gate_up, N_HEADS_B, LANES))
    w = jnp.concatenate([hi, mid, hi, lo, hi, mid], axis=0)
    return jnp.pad(w, ((0, LANES - w.shape[0]), (0, 0)))


SB_T = 256
SB_HEADS = 2
SB_W = SB_HEADS * HEAD_DIM
LOG2E = 1.4426950408889634


def _sb_kernel(q_ref, k_ref, v_ref, gq_ref, gk_ref, cw_ref, o_ref,
               kn_ref, acc_ref, car_ref, zn_ref):
    i = pl.program_id(2)
    heads = [slice(h * HEAD_DIM, (h + 1) * HEAD_DIM) for h in range(SB_HEADS)]

    @pl.when(i == 0)
    def _():
        for hs in heads:
            kn_ref[:, hs] = _rms(k_ref[:, hs].astype(F32), gk_ref[...]).astype(BF16)

    qs = [(_rms(q_ref[:, hs].astype(F32), gq_ref[...]) * (HEAD_DIM ** -0.5 * LOG2E)).astype(BF16)
          for hs in heads]

    def logits(j):
        start = pl.multiple_of(j * SB_T, SB_T)
        return [_dot_t(qs[h], kn_ref[pl.ds(start, SB_T), hs]) for h, hs in enumerate(heads)]

    def block(j, zs, next_slot, diag):
        start = pl.multiple_of(j * SB_T, SB_T)
        if diag:
            strict = (lax.broadcasted_iota(jnp.int32, (SB_T, SB_T), 1)
                      < lax.broadcasted_iota(jnp.int32, (SB_T, SB_T), 0))
        for h, z in enumerate(logits(jnp.maximum(j - 1, 0))):
            zn_ref[next_slot * SB_HEADS + h] = z
        hls = []
        for z in zs:
            sp = jnp.maximum(z, 0.0) + jnp.log2(1.0 + jnp.exp2(-jnp.abs(z)))
            if diag:
                sp = jnp.where(strict, sp, 0.0)
            sp_hi = sp.astype(BF16)
            sp_lo = (sp - sp_hi.astype(F32)).astype(BF16)
            hls.append(jnp.concatenate([sp_hi, sp_lo], axis=1))
        rs = [jnp.dot(hl, cw_ref[...], preferred_element_type=F32) for hl in hls]
        avs = []
        for h in range(SB_HEADS):
            car = car_ref[h]
            a = jnp.exp2(zs[h] - rs[h] - jnp.concatenate([car] * (SB_T // LANES), axis=1))
            if diag:
                a = jnp.where(strict, a, 0.0)
            avs.append(a.astype(BF16))
            car_ref[h] = car + jnp.broadcast_to(rs[h][:, 0:1], car.shape)
        for h, hs in enumerate(heads):
            acc_ref[:, hs] += jnp.dot(avs[h], v_ref[pl.ds(start, SB_T), hs], preferred_element_type=F32)

    acc_ref[...] = jnp.zeros_like(acc_ref)
    car_ref[...] = jnp.zeros_like(car_ref)
    block(i, logits(i), 0, True)

    def body(t, carry):
        slot = t % 2
        zs = [zn_ref[slot * SB_HEADS + h] for h in range(SB_HEADS)]
        block(i - 1 - t, zs, 1 - slot, False)
        return carry

    lax.fori_loop(0, i, body, 0)
    o_ref[...] = acc_ref[...].astype(o_ref.dtype)


def _stick_breaking(qkv, gq, gk):
    bsz, s_len, _ = qkv.shape
    g_spec = pl.BlockSpec((1, HEAD_DIM), lambda b, h, i: (0, 0))
    tri = (jnp.arange(SB_T)[:, None] >= jnp.arange(SB_T)[None, :]).astype(BF16)
    cum_w = jnp.concatenate([tri, tri], axis=0)
    n_grp = N_HEADS_C // SB_HEADS
    return pl.pallas_call(
        _sb_kernel,
        grid=(bsz, n_grp, s_len // SB_T),
        in_specs=[pl.BlockSpec((None, SB_T, SB_W), lambda b, h, i: (b, i, h)),
                  pl.BlockSpec((None, s_len, SB_W), lambda b, h, i: (b, 0, n_grp + h)),
                  pl.BlockSpec((None, s_len, SB_W), lambda b, h, i: (b, 0, 2 * n_grp + h)),
                  g_spec, g_spec,
                  pl.BlockSpec((2 * SB_T, SB_T), lambda b, h, i: (0, 0))],
        out_specs=pl.BlockSpec((None, SB_T, SB_W), lambda b, h, i: (b, i, h)),
        out_shape=jax.ShapeDtypeStruct((bsz, s_len, C_WIDTH), BF16),
        scratch_shapes=[pltpu.VMEM((s_len, SB_W), BF16),
                        pltpu.VMEM((SB_T, SB_W), F32), pltpu.VMEM((SB_HEADS, SB_T, LANES), F32),
                        pltpu.VMEM((2 * SB_HEADS, SB_T, SB_T), F32)],
        compiler_params=_cparams(("parallel", "parallel", "arbitrary")),
        name="stick_breaking",
    )(qkv, qkv, qkv, gq.reshape(1, HEAD_DIM), gk.reshape(1, HEAD_DIM), cum_w)


def _pad_heads(w, n_heads, width):
    lead = w.shape[:-1]
    d = w.shape[-1] // n_heads
    w = w.reshape(*lead, n_heads, d)
    w = jnp.pad(w, [(0, 0)] * len(lead) + [(0, 0), (0, width - d)])
    return w.reshape(*lead, n_heads * width)


def _pad_cols(w, width):
    return jnp.pad(w, [(0, 0)] * (w.ndim - 1) + [(0, width - w.shape[-1])])


def _layer0_weights(w_in):
    parts, o = [], 0
    for s in AB_SIZES:
        parts.append(w_in[:, o:o + s])
        o += s
    qa, ka, va, iq, ik, iw, qb, kb, vb, glow, og = parts
    cols = [qa, ka, va, _pad_heads(qb, N_HEADS_B, LANES), _pad_heads(kb, N_HEADS_B, LANES), vb, og,
            iq, _pad_cols(ik, LANES), _pad_cols(iw, LANES), _pad_cols(glow, 2 * LANES)]
    return jnp.concatenate(cols, axis=1).astype(BF16)


def kernel(x, g_mix, g_ffn, w_in_ab, gq_a, gk_a, w_gate_up, b_gate, g_gla, w_out_ab,
           w_in_c, gq_c, gk_c, w_out_c, w_up, w_down):
    bsz, s_len, d = x.shape
    m = bsz * s_len
    h = x.reshape(m, d)

    p16, p32 = _norm_matmul(h, g_mix[0], _layer0_weights(w_in_ab[0]), tm=PROJ_TM, tn=W512, n16=N_P16)
    p16 = p16.reshape(bsz, s_len, N_P16 * W512)
    p32 = p32.reshape(bsz, s_len, N_P32 * W512)
    tab_a = _rope_tables(s_len, HEAD_DIM, 1)
    tab_i = _rope_tables(s_len, IDX_DIM, LANES // IDX_DIM)
    oa = _dsa(*_prep_a(p16, p32, gq_a[0], gk_a[0], tab_a, tab_i))
    wg = _gla_gate_weights(w_gate_up[0])
    bg = _pad_heads(b_gate[0], N_HEADS_B, LANES).reshape(1, W512)
    ob = _gla(p16, p32, wg, bg, g_gla[0])
    w_o = w_out_ab[0].astype(BF16)
    h = _proj_mlp(h, [oa.reshape(m, A_WIDTH), ob.reshape(m, B_WIDTH)], [w_o[:A_WIDTH], w_o[A_WIDTH:]],
                  g_ffn[0], w_up[0].astype(BF16), w_down[0].astype(BF16), tm=MLP_TM, tf=MLP_TF)

    qkv = _norm_matmul(h, g_mix[1], w_in_c[0].astype(BF16), tm=PROJ_TM, tn=C_WIDTH)
    oc = _stick_breaking(qkv.reshape(bsz, s_len, 3 * C_WIDTH), gq_c[0], gk_c[0])
    h = _proj_mlp(h, [oc.reshape(m, C_WIDTH)], [w_out_c[0].astype(BF16)],
                  g_ffn[1], w_up[1].astype(BF16), w_down[1].astype(BF16), tm=MLP_TM, tf=MLP_TF)
    return h.reshape(bsz, s_len, d)
```

```python
import functools

import numpy as np
import jax
import jax.numpy as jnp
from jax import lax
from jax.experimental import pallas as pl
from jax.experimental.pallas import tpu as pltpu

F32 = jnp.float32
BF16 = jnp.bfloat16

D_MODEL = 1024
CHUNK = 64
HEAD_DIM = 128
EPS = 1e-6
N_HEADS_A = 4
A_WIDTH = N_HEADS_A * HEAD_DIM
IDX_HEADS = 8
IDX_DIM = 64
IDX_TOPK_MAX = 256
N_HEADS_B = 4
GLA_DK = 64
GLA_DV = 128
GLA_GATE_RANK = 16
GLA_TAU = 16.0
B_WIDTH = N_HEADS_B * GLA_DV
N_HEADS_C = 8
C_WIDTH = N_HEADS_C * HEAD_DIM
ROPE_THETA = 500000.0
ROPE_FRACTION = 4
D_FF = 4 * D_MODEL
AB_SIZES = (A_WIDTH, A_WIDTH, A_WIDTH, IDX_HEADS * IDX_DIM, IDX_DIM, IDX_HEADS,
            N_HEADS_B * GLA_DK, N_HEADS_B * GLA_DK, B_WIDTH, GLA_GATE_RANK, B_WIDTH)

LANES = 128
VMEM_LIMIT = 48 * 1024 * 1024
MLP_TM, MLP_TF = 512, 512
PROJ_TM = 512

W512 = 512
P16_QA, P16_KA, P16_VA, P16_QB, P16_KB, P16_VB, P16_OG = range(7)
N_P16 = 7
P32_IQ = 0
P32_IK, P32_IW, P32_GLOW = 4, 5, 6
N_P32 = 2

INT_MIN = np.int32(-2 ** 31)


def _cparams(sem):
    return pltpu.CompilerParams(dimension_semantics=sem, vmem_limit_bytes=VMEM_LIMIT)


def _rms(x, g):
    return x * lax.rsqrt(jnp.mean(x * x, axis=-1, keepdims=True) + EPS) * g


def _log_sigmoid(x):
    return jnp.minimum(x, 0.0) - jnp.log1p(jnp.exp(-jnp.abs(x)))


def _dot_t(a, b):
    return lax.dot_general(a, b, (((1,), (1,)), ((), ())), preferred_element_type=F32)


def _norm_matmul_kernel(n16, tn, x_ref, g_ref, w_ref, *o_refs):
    xn = _rms(x_ref[...], g_ref[...]).astype(BF16)
    for c in range(w_ref.shape[1] // tn):
        y = jnp.dot(xn, w_ref[:, c * tn:(c + 1) * tn], preferred_element_type=F32)
        if c < n16:
            o_refs[0][:, c * tn:(c + 1) * tn] = y.astype(BF16)
        else:
            o_refs[1][:, (c - n16) * tn:(c - n16 + 1) * tn] = y


def _norm_matmul(x, g, w, tm, tn, n16=None):
    m, d = x.shape
    n = w.shape[1]
    split = n16 is not None
    n16 = n16 if split else n // tn
    out_specs = [pl.BlockSpec((tm, n16 * tn), lambda i: (i, 0))]
    out_shape = [jax.ShapeDtypeStruct((m, n16 * tn), BF16)]
    if split:
        out_specs.append(pl.BlockSpec((tm, n - n16 * tn), lambda i: (i, 0)))
        out_shape.append(jax.ShapeDtypeStruct((m, n - n16 * tn), F32))
    out = pl.pallas_call(
        functools.partial(_norm_matmul_kernel, n16, tn),
        grid=(m // tm,),
        in_specs=[pl.BlockSpec((tm, d), lambda i: (i, 0)),
                  pl.BlockSpec((1, d), lambda i: (0, 0)),
                  pl.BlockSpec((d, n), lambda i: (0, 0))],
        out_specs=out_specs,
        out_shape=out_shape,
        compiler_params=_cparams(("parallel",)),
        name="norm_matmul",
    )(x, g.reshape(1, d), w)
    return tuple(out) if split else out[0]


def _proj_mlp_kernel(n_a, tf, h_ref, *refs):
    a_refs = refs[:n_a]
    wo_refs = refs[n_a:2 * n_a]
    g_ref, wu_ref, wd_ref, o_ref, xn_ref = refs[2 * n_a:]
    h1 = h_ref[...]
    for a_ref, wo_ref in zip(a_refs, wo_refs):
        h1 = h1 + jnp.dot(a_ref[...], wo_ref[...], preferred_element_type=F32)
    o_ref[...] = h1
    xn_ref[...] = _rms(h1, g_ref[...]).astype(BF16)
    for c in range(wu_ref.shape[1] // tf):
        u = jnp.dot(xn_ref[...], wu_ref[:, c * tf:(c + 1) * tf], preferred_element_type=F32)
        u = jnp.square(jnp.maximum(u, 0.0)).astype(BF16)
        o_ref[...] += jnp.dot(u, wd_ref[c * tf:(c + 1) * tf, :], preferred_element_type=F32)


def _proj_mlp(h, a_list, wo_list, g, w_up, w_down, tm, tf):
    m, d = h.shape
    n_a = len(a_list)

    def resident(w):
        return pl.BlockSpec(w.shape, lambda i: (0, 0))

    in_specs = [pl.BlockSpec((tm, d), lambda i: (i, 0))]
    in_specs += [pl.BlockSpec((tm, a.shape[1]), lambda i: (i, 0)) for a in a_list]
    in_specs += [resident(w) for w in wo_list]
    in_specs += [pl.BlockSpec((1, d), lambda i: (0, 0)), resident(w_up), resident(w_down)]
    return pl.pallas_call(
        functools.partial(_proj_mlp_kernel, n_a, tf),
        grid=(m // tm,),
        in_specs=in_specs,
        out_specs=pl.BlockSpec((tm, d), lambda i: (i, 0)),
        out_shape=jax.ShapeDtypeStruct((m, d), F32),
        scratch_shapes=[pltpu.VMEM((tm, d), BF16)],
        compiler_params=_cparams(("parallel",)),
        name="proj_mlp",
    )(h, *a_list, *wo_list, g.reshape(1, d), w_up, w_down)


def _rope(x, tab_ref, half):
    out = []
    for s in range(x.shape[1] // LANES):
        xs = x[:, s * LANES:(s + 1) * LANES]
        out.append(xs * tab_ref[0]
                   + pltpu.roll(xs, LANES - half, 1) * tab_ref[1]
                   + pltpu.roll(xs, half, 1) * tab_ref[2])
    return out


def _head_rms(x, g):
    out = []
    for s in range(x.shape[1] // HEAD_DIM):
        out.append(_rms(x[:, s * HEAD_DIM:(s + 1) * HEAD_DIM], g))
    return jnp.concatenate(out, axis=1)


DSA_T = 256
IDX_K = 4 * IDX_DIM


def _hi_lo(x):
    hi = x.astype(BF16)
    return hi, (x - hi.astype(F32)).astype(BF16)


def _prep_a_kernel(qa_ref, ka_ref, va_ref, iq_ref, ik_ref, iw_ref, gq_ref, gk_ref, ta_ref, ti_ref,
                   qt_ref, ko_ref, vt_ref, iqt_ref, iko_ref, iwt_ref):
    half = HEAD_DIM // ROPE_FRACTION // 2
    q = _head_rms(qa_ref[...].astype(F32), gq_ref[...])
    q = jnp.concatenate(_rope(q, ta_ref, half), axis=1) * HEAD_DIM ** -0.5
    qt_ref[...] = q.T.astype(BF16)
    k = _head_rms(ka_ref[...].astype(F32), gk_ref[...])
    ko_ref[...] = jnp.concatenate(_rope(k, ta_ref, half), axis=1).astype(BF16)
    vt_ref[...] = va_ref[...].astype(F32).T.astype(BF16)
    ihalf = IDX_DIM // ROPE_FRACTION // 2
    iq = jnp.concatenate(_rope(iq_ref[...], ti_ref, ihalf), axis=1) * IDX_DIM ** -0.5
    hi, lo = _hi_lo(iq.T)
    for h in range(IDX_HEADS):
        rows = slice(h * IDX_DIM, (h + 1) * IDX_DIM)
        for part, val in enumerate((hi, hi, lo, lo)):
            base = h * IDX_K + part * IDX_DIM
            iqt_ref[base:base + IDX_DIM, :] = val[rows, :]
    ik = _rope(ik_ref[...], ti_ref, ihalf)[0]
    ik_hi, ik_lo = _hi_lo(ik)
    ika = ik_hi.astype(F32) + pltpu.roll(ik_lo.astype(F32), IDX_DIM, 1)
    iko_ref[...] = jnp.concatenate([ika, ika], axis=1).astype(BF16)
    iwt_ref[...] = (iw_ref[...] * IDX_HEADS ** -0.5).T[0:IDX_HEADS, :]


def _prep_a(p16, p32, gq, gk, tab_a, tab_i):
    bsz, s_len, _ = p16.shape
    ts = DSA_T
    n_t = s_len // ts

    def col(c, w):
        return pl.BlockSpec((None, ts, w), lambda s, b, c=c: (b, s, c))

    def rows_out(w):
        return pl.BlockSpec((None, ts, w), lambda s, b: (b, s, 0))

    def tile_out(r):
        return pl.BlockSpec((None, None, r, ts), lambda s, b: (b, s, 0, 0))

    tab_spec = pl.BlockSpec((3, ts, LANES), lambda s, b: (0, s, 0))
    g_spec = pl.BlockSpec((1, HEAD_DIM), lambda s, b: (0, 0))
    sd = jax.ShapeDtypeStruct
    return pl.pallas_call(
        _prep_a_kernel,
        grid=(n_t, bsz),
        in_specs=[col(P16_QA, W512), col(P16_KA, W512), col(P16_VA, W512), col(P32_IQ, W512),
                  col(P32_IK, LANES), col(P32_IW, LANES), g_spec, g_spec, tab_spec, tab_spec],
        out_specs=[tile_out(A_WIDTH), rows_out(A_WIDTH), tile_out(A_WIDTH), tile_out(IDX_HEADS * IDX_K),
                   rows_out(IDX_K), tile_out(IDX_HEADS)],
        out_shape=[sd((bsz, n_t, A_WIDTH, ts), BF16), sd((bsz, s_len, A_WIDTH), BF16),
                   sd((bsz, n_t, A_WIDTH, ts), BF16), sd((bsz, n_t, IDX_HEADS * IDX_K, ts), BF16),
                   sd((bsz, s_len, IDX_K), BF16), sd((bsz, n_t, IDX_HEADS, ts), F32)],
        compiler_params=_cparams(("parallel", "parallel")),
        name="prep_a",
    )(p16, p16, p16, p32, p32, p32, gq.reshape(1, HEAD_DIM), gk.reshape(1, HEAD_DIM), tab_a, tab_i)


def _rope_tables(s_len, period, reps):
    rot = period // ROPE_FRACTION
    half = rot // 2
    inv = jnp.power(ROPE_THETA, -jnp.arange(half, dtype=F32) * 2.0 / rot)
    ang = jnp.arange(s_len).astype(F32)[:, None] * inv[None, :]
    cos, sin = jnp.cos(ang), jnp.sin(ang)
    pad = jnp.zeros((s_len, period - rot), F32)
    zero = jnp.zeros((s_len, half), F32)
    c = jnp.concatenate([cos, cos, pad + 1.0], axis=1)
    s1 = jnp.concatenate([-sin, zero, pad], axis=1)
    s2 = jnp.concatenate([zero, sin, pad], axis=1)
    return jnp.stack([jnp.tile(t, (1, reps)) for t in (c, s1, s2)])


INT_MAX = np.int32(2 ** 31 - 1)
MASKED = -1e30
SUBLANES = 8


def _dsa_kernel(topk, qt_ref, iqt_ref, iwt_ref, k_ref, vt_ref, ik_ref, o_ref,
                key_ref, acc_ref, jb_ref, v_ref, nge_ref):
    t = DSA_T
    qi = pl.program_id(1)
    n_kt = qi + 1
    s_len = k_ref.shape[0]
    qpos = lax.broadcasted_iota(jnp.int32, (1, t), 1)
    limit = qi * t + (qpos // CHUNK + 1) * CHUNK
    krow = lax.broadcasted_iota(jnp.int32, (t, t), 0)

    def score_tile(kt, carry):
        start = pl.multiple_of(kt * t, t)
        ikt = ik_ref[pl.ds(start, t), :]
        score = jnp.zeros((t, t), F32)
        for h in range(IDX_HEADS):
            logit = jnp.dot(ikt, iqt_ref[h * IDX_K:(h + 1) * IDX_K, :], preferred_element_type=F32)
            score = score + jnp.maximum(logit, 0.0) * iwt_ref[h:h + 1, :]
        bits = pltpu.bitcast(score, jnp.int32)
        key = bits ^ ((bits >> 31) & jnp.int32(0x7FFFFFFF))
        key_ref[kt] = jnp.where(krow + start < limit, key, INT_MIN)
        return carry

    lax.fori_loop(0, n_kt, score_tile, 0)

    def count(pred, n=None):
        def body(kt, acc):
            ones = jnp.where(pred(kt), 1.0, 0.0)
            return acc + jnp.sum(ones.reshape(t // SUBLANES, SUBLANES, t), axis=0)
        acc = jnp.zeros((SUBLANES, t), F32)
        if n is None:
            acc = lax.fori_loop(0, n_kt, body, acc)
        else:
            for kt in range(n):
                acc = body(kt, acc)
        return jnp.sum(acc, axis=0, keepdims=True)

    for n in range(1, key_ref.shape[0] + 1):
        @pl.when(n_kt == n)
        def _(n=n):
            def vbody(it, carry):
                lo, n_lo = carry
                cand = lo + lax.shift_left(jnp.int32(1), 31 - it)
                cnt = count(lambda kt: key_ref[kt] >= cand, n)
                ok = cnt >= topk
                return jnp.where(ok, cand, lo), jnp.where(ok, cnt, n_lo)

            lo, n_lo = lax.fori_loop(0, 32, vbody, (jnp.full((1, t), INT_MIN, jnp.int32),
                                                    jnp.full((1, t), float(n * t), F32)))
            v_ref[...] = jnp.broadcast_to(lo, v_ref.shape)
            nge_ref[...] = jnp.broadcast_to(n_lo, nge_ref.shape)

    v = v_ref[0:1, :]
    n_ge = nge_ref[0:1, :]

    jb_ref[...] = jnp.full(jb_ref.shape, INT_MAX, jnp.int32)

    @pl.when(jnp.max(n_ge) > topk)
    def _():
        need = topk - count(lambda kt: key_ref[kt] > v)
        nbits = (s_len - 1).bit_length()

        def jbody(it, jb):
            cand = jb + lax.shift_left(jnp.int32(1), nbits - 1 - it)
            few = count(lambda kt: (key_ref[kt] == v) & (krow + kt * t < cand)) < need
            return jnp.where(few, cand, jb)

        jb = lax.fori_loop(0, nbits, jbody, jnp.zeros((1, t), jnp.int32))
        jb_ref[...] = jnp.broadcast_to(jb, jb_ref.shape)

    jb = jnp.where(v == INT_MIN, -1, jb_ref[0:1, :])

    acc_ref[...] = jnp.zeros_like(acc_ref)

    def att_tile(kt, carry):
        ms, ls = carry
        start = pl.multiple_of(kt * t, t)
        key = key_ref[kt]
        tie_bias = jnp.where(key == v, jnp.where(krow + start <= jb, 0.0, MASKED), MASKED)
        bias = jnp.where(key > v, 0.0, tie_bias)
        heads = [slice(h * HEAD_DIM, (h + 1) * HEAD_DIM) for h in range(N_HEADS_A)]
        ss = [jnp.dot(k_ref[pl.ds(start, t), hs], qt_ref[hs, :], preferred_element_type=F32) + bias
              for hs in heads]
        new_ms = [jnp.maximum(ms[h], jnp.max(ss[h], axis=0, keepdims=True)) for h in range(N_HEADS_A)]
        ps = [jnp.exp(ss[h] - new_ms[h]) for h in range(N_HEADS_A)]
        alphas = [jnp.exp(ms[h] - new_ms[h]) for h in range(N_HEADS_A)]
        new_ls = [alphas[h] * ls[h] + jnp.sum(ps[h], axis=0, keepdims=True) for h in range(N_HEADS_A)]
        pvs = [jnp.dot(vt_ref[kt, hs, :], ps[h].astype(BF16), preferred_element_type=F32)
               for h, hs in enumerate(heads)]
        for h in range(N_HEADS_A):
            acc_ref[h] = alphas[h] * acc_ref[h] + pvs[h]
        return tuple(new_ms), tuple(new_ls)

    init = (tuple(jnp.full((1, t), MASKED, F32) for _ in range(N_HEADS_A)),
            tuple(jnp.zeros((1, t), F32) for _ in range(N_HEADS_A)))
    _, ls = lax.fori_loop(0, n_kt, att_tile, init)
    for h in range(N_HEADS_A):
        o_ref[:, h * HEAD_DIM:(h + 1) * HEAD_DIM] = (acc_ref[h] / ls[h]).T.astype(o_ref.dtype)


def _dsa(qt, ka, vt, iqt, ik_ext, iwt):
    bsz, s_len, _ = ka.shape
    t = DSA_T
    n_t = s_len // t
    topk = min(IDX_TOPK_MAX, s_len // 4)

    def qtile(r):
        return pl.BlockSpec((None, None, r, t), lambda b, i: (b, i, 0, 0))

    return pl.pallas_call(
        functools.partial(_dsa_kernel, topk),
        grid=(bsz, n_t),
        in_specs=[qtile(A_WIDTH), qtile(IDX_HEADS * IDX_K), qtile(IDX_HEADS),
                  pl.BlockSpec((None, s_len, A_WIDTH), lambda b, i: (b, 0, 0)),
                  pl.BlockSpec((None, n_t, A_WIDTH, t), lambda b, i: (b, 0, 0, 0)),
                  pl.BlockSpec((None, s_len, IDX_K), lambda b, i: (b, 0, 0))],
        out_specs=pl.BlockSpec((None, t, A_WIDTH), lambda b, i: (b, i, 0)),
        out_shape=jax.ShapeDtypeStruct((bsz, s_len, A_WIDTH), BF16),
        scratch_shapes=[pltpu.VMEM((n_t, t, t), jnp.int32),
                        pltpu.VMEM((N_HEADS_A, HEAD_DIM, t), F32),
                        pltpu.VMEM((SUBLANES, t), jnp.int32),
                        pltpu.VMEM((SUBLANES, t), jnp.int32),
                        pltpu.VMEM((SUBLANES, t), F32)],
        compiler_params=_cparams(("parallel", "arbitrary")),
        name="dsa",
    )(qt, iqt, iwt, ka, vt, ik_ext)


GLA_ROWS = 256


GLA_BLK = SUBLANES
GLA_NBLK = CHUNK // GLA_BLK
GLA_PAIRS = [(i, j) for i in range(GLA_NBLK) for j in range(i)]


def _split3(x):
    hi = x.astype(BF16)
    r1 = x - hi.astype(F32)
    mid = r1.astype(BF16)
    return hi, mid, (r1 - mid.astype(F32)).astype(BF16)


def _gla_cumsum_matrix():
    t = np.arange(CHUNK)
    blk = t // GLA_BLK
    incl = t[None, :] <= t[:, None]
    start = t[None, :] < (GLA_BLK * blk)[:, None]
    end = t[None, :] < (GLA_BLK * (blk + 1))[:, None]
    bounds = t[None, :] < (GLA_BLK * np.arange(GLA_NBLK))[:, None]
    pad = np.zeros((GLA_BLK, CHUNK), bool)
    m = np.concatenate([incl, start, end, bounds, pad], axis=0).astype(np.float32)
    return jnp.asarray(np.concatenate([m, m, m], axis=1), BF16)


def _gla_kernel(q_ref, k_ref, v_ref, og_ref, gl_ref, wg_ref, bg_ref, gn_ref, cm_ref, o_ref,
                st_ref, lg_ref, b_ref, kc_ref):
    @pl.when(pl.program_id(1) == 0)
    def _():
        st_ref[...] = jnp.zeros_like(st_ref)

    g_hi, g_mid, g_lo = (p.astype(F32) for p in _split3(gl_ref[...]))
    r = GLA_GATE_RANK
    packed = (g_hi + pltpu.roll(g_hi, r, 1) + pltpu.roll(g_mid, 2 * r, 1) + pltpu.roll(g_hi, 3 * r, 1)
              + pltpu.roll(g_lo, 4 * r, 1) + pltpu.roll(g_mid, 5 * r, 1))
    gate = jnp.dot(packed.astype(BF16), wg_ref[...], preferred_element_type=F32) + bg_ref[...]
    lg_ref[...] = _log_sigmoid(gate) * (1.0 / GLA_TAU)

    sub = lax.broadcasted_iota(jnp.int32, (GLA_BLK, LANES), 0)
    lane_c = lax.broadcasted_iota(jnp.int32, (GLA_BLK, CHUNK), 1)
    heads = [slice(h * LANES, (h + 1) * LANES) for h in range(N_HEADS_B)]

    def chunk(ci, carry):
        r0 = pl.multiple_of(ci * CHUNK, CHUNK)
        rows = pl.ds(r0, CHUNK)
        cums = []
        for hs in heads:
            parts = _split3(lg_ref[rows, hs])
            cums.append(jnp.dot(cm_ref[...], jnp.concatenate(parts, axis=0), preferred_element_type=F32))
        for h, hs in enumerate(heads):
            b_ref[:, hs] = cums[h][0:CHUNK]
        kc_ref[...] = k_ref[rows, :].astype(F32)
        atts, qes, kds, bls = [], [], [], []
        for h, hs in enumerate(heads):
            b = cums[h][0:CHUNK]
            b_start = cums[h][CHUNK:2 * CHUNK]
            b_end = cums[h][2 * CHUNK:3 * CHUNK]
            bound = cums[h][3 * CHUNK:3 * CHUNK + GLA_NBLK]
            q = q_ref[rows, hs].astype(F32) * GLA_DK ** -0.5
            k = kc_ref[:, hs]
            qb = q * jnp.exp(b - b_start)
            kb = (k * jnp.exp(b_end - b)).astype(BF16)
            dec = [jnp.exp(jnp.where(sub > j, bound - bound[j + 1:j + 2], -jnp.inf))
                   for j in range(GLA_NBLK - 1)]
            lhs = jnp.concatenate([qb[i * GLA_BLK:(i + 1) * GLA_BLK] * dec[j][i:i + 1]
                                   for i, j in GLA_PAIRS], axis=0).astype(BF16)
            cross = _dot_t(lhs, kb)
            att_rows = []
            for i in range(GLA_NBLK):
                att = jnp.zeros((GLA_BLK, CHUNK), F32)
                for g, (gi, j) in enumerate(GLA_PAIRS):
                    if gi == i:
                        att = jnp.where(lane_c // GLA_BLK == j, cross[g * GLA_BLK:(g + 1) * GLA_BLK], att)
                blk = slice(i * GLA_BLK, (i + 1) * GLA_BLK)
                qi, bi = q[blk], b[blk]
                for sl in range(GLA_BLK):
                    srow = pl.ds(i * GLA_BLK + sl, 1)
                    ks = jnp.broadcast_to(kc_ref[srow, hs], (GLA_BLK, LANES))
                    bs = jnp.broadcast_to(b_ref[srow, hs], (GLA_BLK, LANES))
                    e = jnp.exp(jnp.where(sub >= sl, bi - bs, -jnp.inf))
                    col = jnp.sum(qi * ks * e, axis=1, keepdims=True)
                    att = jnp.where(lane_c == i * GLA_BLK + sl, col, att)
                att_rows.append(att)
            atts.append(jnp.concatenate(att_rows, axis=0).astype(BF16))
            bl = b[CHUNK - 1:CHUNK]
            qes.append((q * jnp.exp(b)).astype(BF16))
            kds.append((k * jnp.exp(bl - b)).astype(BF16))
            bls.append(bl)
        for h, hs in enumerate(heads):
            v = v_ref[rows, hs]
            st = st_ref[h]
            o = _dot_t(qes[h], st.astype(BF16)) + jnp.dot(atts[h], v, preferred_element_type=F32)
            st_ref[h] = st * jnp.exp(bls[h]) + lax.dot_general(
                v, kds[h], (((0,), (0,)), ((), ())), preferred_element_type=F32)
            og = og_ref[rows, hs].astype(F32)
            o_ref[rows, hs] = (_rms(o, gn_ref[...]) * (og * jax.nn.sigmoid(og))).astype(o_ref.dtype)
        return carry

    lax.fori_loop(0, q_ref.shape[0] // CHUNK, chunk, 0)


def _gla(p16, p32, wg, bg, gn):
    bsz, s_len, _ = p16.shape

    def col(c, w):
        return pl.BlockSpec((None, GLA_ROWS, w), lambda b, s, c=c: (b, s, c))

    cm = _gla_cumsum_matrix()
    return pl.pallas_call(
        _gla_kernel,
        grid=(bsz, s_len // GLA_ROWS),
        in_specs=[col(P16_QB, W512), col(P16_KB, W512), col(P16_VB, W512), col(P16_OG, W512),
                  col(P32_GLOW, LANES),
                  pl.BlockSpec((LANES, W512), lambda b, s: (0, 0)),
                  pl.BlockSpec((1, W512), lambda b, s: (0, 0)),
                  pl.BlockSpec((1, GLA_DV), lambda b, s: (0, 0)),
                  pl.BlockSpec(cm.shape, lambda b, s: (0, 0))],
        out_specs=pl.BlockSpec((None, GLA_ROWS, B_WIDTH), lambda b, s: (b, s, 0)),
        out_shape=jax.ShapeDtypeStruct((bsz, s_len, B_WIDTH), BF16),
        scratch_shapes=[pltpu.VMEM((N_HEADS_B, GLA_DV, LANES), F32),
                        pltpu.VMEM((GLA_ROWS, W512), F32),
                        pltpu.VMEM((CHUNK, W512), F32),
                        pltpu.VMEM((CHUNK, W512), F32)],
        compiler_params=_cparams(("parallel", "arbitrary")),
        name="gla",
    )(p16, p16, p16, p16, p32, wg, bg, gn.reshape(1, GLA_DV), cm)


def _gla_gate_weights(w_gate_up):
    hi, mid, lo = _split3(_pad_heads(w_gate_up, N_HEADS_B, LANES))
    w = jnp.concatenate([hi, mid, hi, lo, hi, mid], axis=0)
    return jnp.pad(w, ((0, LANES - w.shape[0]), (0, 0)))


SB_T = 256
SB_HEADS = 2
SB_W = SB_HEADS * HEAD_DIM
LOG2E = 1.4426950408889634


def _sb_kernel(q_ref, k_ref, v_ref, gq_ref, gk_ref, cw_ref, o_ref,
               kn_ref, acc_ref, car_ref, zn_ref):
    i = pl.program_id(2)
    heads = [slice(h * HEAD_DIM, (h + 1) * HEAD_DIM) for h in range(SB_HEADS)]

    @pl.when(i == 0)
    def _():
        for hs in heads:
            kn_ref[:, hs] = _rms(k_ref[:, hs].astype(F32), gk_ref[...]).astype(BF16)

    qs = [(_rms(q_ref[:, hs].astype(F32), gq_ref[...]) * (HEAD_DIM ** -0.5 * LOG2E)).astype(BF16)
          for hs in heads]

    def logits(j):
        start = pl.multiple_of(j * SB_T, SB_T)
        return [_dot_t(qs[h], kn_ref[pl.ds(start, SB_T), hs]) for h, hs in enumerate(heads)]

    def block(j, zs, next_slot, diag):
        start = pl.multiple_of(j * SB_T, SB_T)
        if diag:
            strict = (lax.broadcasted_iota(jnp.int32, (SB_T, SB_T), 1)
                      < lax.broadcasted_iota(jnp.int32, (SB_T, SB_T), 0))
        for h, z in enumerate(logits(jnp.maximum(j - 1, 0))):
            zn_ref[next_slot * SB_HEADS + h] = z
        hls = []
        for z in zs:
            sp = jnp.maximum(z, 0.0) + jnp.log2(1.0 + jnp.exp2(-jnp.abs(z)))
            if diag:
                sp = jnp.where(strict, sp, 0.0)
            sp_hi = sp.astype(BF16)
            sp_lo = (sp - sp_hi.astype(F32)).astype(BF16)
            hls.append(jnp.concatenate([sp_hi, sp_lo], axis=1))
        rs = [jnp.dot(hl, cw_ref[...], preferred_element_type=F32) for hl in hls]
        avs = []
        for h in range(SB_HEADS):
            car = car_ref[h]
            a = jnp.exp2(zs[h] - rs[h] - jnp.concatenate([car] * (SB_T // LANES), axis=1))
            if diag:
                a = jnp.where(strict, a, 0.0)
            avs.append(a.astype(BF16))
            car_ref[h] = car + jnp.broadcast_to(rs[h][:, 0:1], car.shape)
        for h, hs in enumerate(heads):
            acc_ref[:, hs] += jnp.dot(avs[h], v_ref[pl.ds(start, SB_T), hs], preferred_element_type=F32)

    acc_ref[...] = jnp.zeros_like(acc_ref)
    car_ref[...] = jnp.zeros_like(car_ref)
    block(i, logits(i), 0, True)

    def trip(j, slot):
        zs = [zn_ref[slot * SB_HEADS + h] for h in range(SB_HEADS)]
        block(j, zs, 1 - slot, False)

    def body(u, carry):
        j = i - 1 - 2 * u
        trip(j, 0)
        trip(j - 1, 1)
        return carry

    lax.fori_loop(0, i // 2, body, 0)

    @pl.when(i % 2 == 1)
    def _():
        trip(0, 0)

    o_ref[...] = acc_ref[...].astype(o_ref.dtype)


def _stick_breaking(qkv, gq, gk):
    bsz, s_len, _ = qkv.shape
    g_spec = pl.BlockSpec((1, HEAD_DIM), lambda b, h, i: (0, 0))
    tri = (jnp.arange(SB_T)[:, None] >= jnp.arange(SB_T)[None, :]).astype(BF16)
    cum_w = jnp.concatenate([tri, tri], axis=0)
    n_grp = N_HEADS_C // SB_HEADS
    return pl.pallas_call(
        _sb_kernel,
        grid=(bsz, n_grp, s_len // SB_T),
        in_specs=[pl.BlockSpec((None, SB_T, SB_W), lambda b, h, i: (b, i, h)),
                  pl.BlockSpec((None, s_len, SB_W), lambda b, h, i: (b, 0, n_grp + h)),
                  pl.BlockSpec((None, s_len, SB_W), lambda b, h, i: (b, 0, 2 * n_grp + h)),
                  g_spec, g_spec,
                  pl.BlockSpec((2 * SB_T, SB_T), lambda b, h, i: (0, 0))],
        out_specs=pl.BlockSpec((None, SB_T, SB_W), lambda b, h, i: (b, i, h)),
        out_shape=jax.ShapeDtypeStruct((bsz, s_len, C_WIDTH), BF16),
        scratch_shapes=[pltpu.VMEM((s_len, SB_W), BF16),
                        pltpu.VMEM((SB_T, SB_W), F32), pltpu.VMEM((SB_HEADS, SB_T, LANES), F32),
                        pltpu.VMEM((2 * SB_HEADS, SB_T, SB_T), F32)],
        compiler_params=_cparams(("parallel", "parallel", "arbitrary")),
        name="stick_breaking",
    )(qkv, qkv, qkv, gq.reshape(1, HEAD_DIM), gk.reshape(1, HEAD_DIM), cum_w)


def _pad_heads(w, n_heads, width):
    lead = w.shape[:-1]
    d = w.shape[-1] // n_heads
    w = w.reshape(*lead, n_heads, d)
    w = jnp.pad(w, [(0, 0)] * len(lead) + [(0, 0), (0, width - d)])
    return w.reshape(*lead, n_heads * width)


def _pad_cols(w, width):
    return jnp.pad(w, [(0, 0)] * (w.ndim - 1) + [(0, width - w.shape[-1])])


def _layer0_weights(w_in):
    parts, o = [], 0
    for s in AB_SIZES:
        parts.append(w_in[:, o:o + s])
        o += s
    qa, ka, va, iq, ik, iw, qb, kb, vb, glow, og = parts
    cols = [qa, ka, va, _pad_heads(qb, N_HEADS_B, LANES), _pad_heads(kb, N_HEADS_B, LANES), vb, og,
            iq, _pad_cols(ik, LANES), _pad_cols(iw, LANES), _pad_cols(glow, 2 * LANES)]
    return jnp.concatenate(cols, axis=1).astype(BF16)


def kernel(x, g_mix, g_ffn, w_in_ab, gq_a, gk_a, w_gate_up, b_gate, g_gla, w_out_ab,
           w_in_c, gq_c, gk_c, w_out_c, w_up, w_down):
    bsz, s_len, d = x.shape
    m = bsz * s_len
    h = x.reshape(m, d)

    p16, p32 = _norm_matmul(h, g_mix[0], _layer0_weights(w_in_ab[0]), tm=PROJ_TM, tn=W512, n16=N_P16)
    p16 = p16.reshape(bsz, s_len, N_P16 * W512)
    p32 = p32.reshape(bsz, s_len, N_P32 * W512)
    tab_a = _rope_tables(s_len, HEAD_DIM, 1)
    tab_i = _rope_tables(s_len, IDX_DIM, LANES // IDX_DIM)
    oa = _dsa(*_prep_a(p16, p32, gq_a[0], gk_a[0], tab_a, tab_i))
    wg = _gla_gate_weights(w_gate_up[0])
    bg = _pad_heads(b_gate[0], N_HEADS_B, LANES).reshape(1, W512)
    ob = _gla(p16, p32, wg, bg, g_gla[0])
    w_o = w_out_ab[0].astype(BF16)
    h = _proj_mlp(h, [oa.reshape(m, A_WIDTH), ob.reshape(m, B_WIDTH)], [w_o[:A_WIDTH], w_o[A_WIDTH:]],
                  g_ffn[0], w_up[0].astype(BF16), w_down[0].astype(BF16), tm=MLP_TM, tf=MLP_TF)

    qkv = _norm_matmul(h, g_mix[1], w_in_c[0].astype(BF16), tm=PROJ_TM, tn=C_WIDTH)
    oc = _stick_breaking(qkv.reshape(bsz, s_len, 3 * C_WIDTH), gq_c[0], gk_c[0])
    h = _proj_mlp(h, [oc.reshape(m, C_WIDTH)], [w_out_c[0].astype(BF16)],
                  g_ffn[1], w_up[1].astype(BF16), w_down[1].astype(BF16), tm=MLP_TM, tf=MLP_TF)
    return h.reshape(bsz, s_len, d)
```

```python
import functools

import numpy as np
import jax
import jax.numpy as jnp
from jax import lax
from jax.experimental import pallas as pl
from jax.experimental.pallas import tpu as pltpu

F32 = jnp.float32
BF16 = jnp.bfloat16

D_MODEL = 1024
CHUNK = 64
HEAD_DIM = 128
EPS = 1e-6
N_HEADS_A = 4
A_WIDTH = N_HEADS_A * HEAD_DIM
IDX_HEADS = 8
IDX_DIM = 64
IDX_TOPK_MAX = 256
N_HEADS_B = 4
GLA_DK = 64
GLA_DV = 128
GLA_GATE_RANK = 16
GLA_TAU = 16.0
B_WIDTH = N_HEADS_B * GLA_DV
N_HEADS_C = 8
C_WIDTH = N_HEADS_C * HEAD_DIM
ROPE_THETA = 500000.0
ROPE_FRACTION = 4
D_FF = 4 * D_MODEL
AB_SIZES = (A_WIDTH, A_WIDTH, A_WIDTH, IDX_HEADS * IDX_DIM, IDX_DIM, IDX_HEADS,
            N_HEADS_B * GLA_DK, N_HEADS_B * GLA_DK, B_WIDTH, GLA_GATE_RANK, B_WIDTH)

LANES = 128
VMEM_LIMIT = 48 * 1024 * 1024
MLP_TM, MLP_TF = 512, 512
PROJ_TM = 512

W512 = 512
P16_QA, P16_KA, P16_VA, P16_QKB, P16_VB, P16_OG = range(6)
N_P16 = 6
P32_IQ = 0
P32_IK, P32_IW, P32_GLOW = 4, 5, 6
N_P32 = 2

INT_MIN = np.int32(-2 ** 31)


def _cparams(sem):
    return pltpu.CompilerParams(dimension_semantics=sem, vmem_limit_bytes=VMEM_LIMIT)


def _rms(x, g):
    return x * lax.rsqrt(jnp.mean(x * x, axis=-1, keepdims=True) + EPS) * g


def _log_sigmoid(x):
    return jnp.minimum(x, 0.0) - jnp.log1p(jnp.exp(-jnp.abs(x)))


def _dot_t(a, b):
    return lax.dot_general(a, b, (((1,), (1,)), ((), ())), preferred_element_type=F32)


def _norm_matmul_kernel(n16, tn, x_ref, g_ref, w_ref, *o_refs):
    xn = _rms(x_ref[...], g_ref[...]).astype(BF16)
    for c in range(w_ref.shape[1] // tn):
        y = jnp.dot(xn, w_ref[:, c * tn:(c + 1) * tn], preferred_element_type=F32)
        if c < n16:
            o_refs[0][:, c * tn:(c + 1) * tn] = y.astype(BF16)
        else:
            o_refs[1][:, (c - n16) * tn:(c - n16 + 1) * tn] = y


def _norm_matmul(x, g, w, tm, tn, n16=None):
    m, d = x.shape
    n = w.shape[1]
    split = n16 is not None
    n16 = n16 if split else n // tn
    out_specs = [pl.BlockSpec((tm, n16 * tn), lambda i: (i, 0))]
    out_shape = [jax.ShapeDtypeStruct((m, n16 * tn), BF16)]
    if split:
        out_specs.append(pl.BlockSpec((tm, n - n16 * tn), lambda i: (i, 0)))
        out_shape.append(jax.ShapeDtypeStruct((m, n - n16 * tn), F32))
    out = pl.pallas_call(
        functools.partial(_norm_matmul_kernel, n16, tn),
        grid=(m // tm,),
        in_specs=[pl.BlockSpec((tm, d), lambda i: (i, 0)),
                  pl.BlockSpec((1, d), lambda i: (0, 0)),
                  pl.BlockSpec((d, n), lambda i: (0, 0))],
        out_specs=out_specs,
        out_shape=out_shape,
        compiler_params=_cparams(("parallel",)),
        name="norm_matmul",
    )(x, g.reshape(1, d), w)
    return tuple(out) if split else out[0]


def _proj_mlp_kernel(n_a, tf, h_ref, *refs):
    a_refs = refs[:n_a]
    wo_refs = refs[n_a:2 * n_a]
    g_ref, wu_ref, wd_ref, o_ref, xn_ref = refs[2 * n_a:]
    h1 = h_ref[...]
    for a_ref, wo_ref in zip(a_refs, wo_refs):
        h1 = h1 + jnp.dot(a_ref[...], wo_ref[...], preferred_element_type=F32)
    o_ref[...] = h1
    xn_ref[...] = _rms(h1, g_ref[...]).astype(BF16)
    for c in range(wu_ref.shape[1] // tf):
        u = jnp.dot(xn_ref[...], wu_ref[:, c * tf:(c + 1) * tf], preferred_element_type=F32)
        u = jnp.square(jnp.maximum(u, 0.0)).astype(BF16)
        o_ref[...] += jnp.dot(u, wd_ref[c * tf:(c + 1) * tf, :], preferred_element_type=F32)


def _proj_mlp(h, a_list, wo_list, g, w_up, w_down, tm, tf):
    m, d = h.shape
    n_a = len(a_list)

    def resident(w):
        return pl.BlockSpec(w.shape, lambda i: (0, 0))

    in_specs = [pl.BlockSpec((tm, d), lambda i: (i, 0))]
    in_specs += [pl.BlockSpec((tm, a.shape[1]), lambda i: (i, 0)) for a in a_list]
    in_specs += [resident(w) for w in wo_list]
    in_specs += [pl.BlockSpec((1, d), lambda i: (0, 0)), resident(w_up), resident(w_down)]
    return pl.pallas_call(
        functools.partial(_proj_mlp_kernel, n_a, tf),
        grid=(m // tm,),
        in_specs=in_specs,
        out_specs=pl.BlockSpec((tm, d), lambda i: (i, 0)),
        out_shape=jax.ShapeDtypeStruct((m, d), F32),
        scratch_shapes=[pltpu.VMEM((tm, d), BF16)],
        compiler_params=_cparams(("parallel",)),
        name="proj_mlp",
    )(h, *a_list, *wo_list, g.reshape(1, d), w_up, w_down)


def _rope(x, tab_ref, half):
    out = []
    for s in range(x.shape[1] // LANES):
        xs = x[:, s * LANES:(s + 1) * LANES]
        out.append(xs * tab_ref[0]
                   + pltpu.roll(xs, LANES - half, 1) * tab_ref[1]
                   + pltpu.roll(xs, half, 1) * tab_ref[2])
    return out


def _head_rms(x, g):
    out = []
    for s in range(x.shape[1] // HEAD_DIM):
        out.append(_rms(x[:, s * HEAD_DIM:(s + 1) * HEAD_DIM], g))
    return jnp.concatenate(out, axis=1)


DSA_T = 256
IDX_K = 4 * IDX_DIM


def _hi_lo(x):
    hi = x.astype(BF16)
    return hi, (x - hi.astype(F32)).astype(BF16)


def _prep_a_kernel(qa_ref, ka_ref, va_ref, iq_ref, ik_ref, iw_ref, gq_ref, gk_ref, ta_ref, ti_ref,
                   qt_ref, ko_ref, vt_ref, iqt_ref, iko_ref, iwt_ref):
    half = HEAD_DIM // ROPE_FRACTION // 2
    q = _head_rms(qa_ref[...].astype(F32), gq_ref[...])
    q = jnp.concatenate(_rope(q, ta_ref, half), axis=1) * HEAD_DIM ** -0.5
    qt_ref[...] = q.T.astype(BF16)
    k = _head_rms(ka_ref[...].astype(F32), gk_ref[...])
    ko_ref[...] = jnp.concatenate(_rope(k, ta_ref, half), axis=1).astype(BF16)
    vt_ref[...] = va_ref[...].astype(F32).T.astype(BF16)
    ihalf = IDX_DIM // ROPE_FRACTION // 2
    iq = jnp.concatenate(_rope(iq_ref[...], ti_ref, ihalf), axis=1) * IDX_DIM ** -0.5
    hi, lo = _hi_lo(iq.T)
    for h in range(IDX_HEADS):
        rows = slice(h * IDX_DIM, (h + 1) * IDX_DIM)
        for part, val in enumerate((hi, hi, lo, lo)):
            base = h * IDX_K + part * IDX_DIM
            iqt_ref[base:base + IDX_DIM, :] = val[rows, :]
    ik = _rope(ik_ref[...], ti_ref, ihalf)[0]
    ik_hi, ik_lo = _hi_lo(ik)
    ika = ik_hi.astype(F32) + pltpu.roll(ik_lo.astype(F32), IDX_DIM, 1)
    iko_ref[...] = jnp.concatenate([ika, ika], axis=1).astype(BF16)
    iwt_ref[...] = (iw_ref[...] * IDX_HEADS ** -0.5).T[0:IDX_HEADS, :]


def _prep_a(p16, p32, gq, gk, tab_a, tab_i):
    bsz, s_len, _ = p16.shape
    ts = DSA_T
    n_t = s_len // ts

    def col(c, w):
        return pl.BlockSpec((None, ts, w), lambda s, b, c=c: (b, s, c))

    def rows_out(w):
        return pl.BlockSpec((None, ts, w), lambda s, b: (b, s, 0))

    def tile_out(r):
        return pl.BlockSpec((None, None, r, ts), lambda s, b: (b, s, 0, 0))

    tab_spec = pl.BlockSpec((3, ts, LANES), lambda s, b: (0, s, 0))
    g_spec = pl.BlockSpec((1, HEAD_DIM), lambda s, b: (0, 0))
    sd = jax.ShapeDtypeStruct
    return pl.pallas_call(
        _prep_a_kernel,
        grid=(n_t, bsz),
        in_specs=[col(P16_QA, W512), col(P16_KA, W512), col(P16_VA, W512), col(P32_IQ, W512),
                  col(P32_IK, LANES), col(P32_IW, LANES), g_spec, g_spec, tab_spec, tab_spec],
        out_specs=[tile_out(A_WIDTH), rows_out(A_WIDTH), tile_out(A_WIDTH), tile_out(IDX_HEADS * IDX_K),
                   rows_out(IDX_K), tile_out(IDX_HEADS)],
        out_shape=[sd((bsz, n_t, A_WIDTH, ts), BF16), sd((bsz, s_len, A_WIDTH), BF16),
                   sd((bsz, n_t, A_WIDTH, ts), BF16), sd((bsz, n_t, IDX_HEADS * IDX_K, ts), BF16),
                   sd((bsz, s_len, IDX_K), BF16), sd((bsz, n_t, IDX_HEADS, ts), F32)],
        compiler_params=_cparams(("parallel", "parallel")),
        name="prep_a",
    )(p16, p16, p16, p32, p32, p32, gq.reshape(1, HEAD_DIM), gk.reshape(1, HEAD_DIM), tab_a, tab_i)


def _rope_tables(s_len, period, reps):
    rot = period // ROPE_FRACTION
    half = rot // 2
    inv = jnp.power(ROPE_THETA, -jnp.arange(half, dtype=F32) * 2.0 / rot)
    ang = jnp.arange(s_len).astype(F32)[:, None] * inv[None, :]
    cos, sin = jnp.cos(ang), jnp.sin(ang)
    pad = jnp.zeros((s_len, period - rot), F32)
    zero = jnp.zeros((s_len, half), F32)
    c = jnp.concatenate([cos, cos, pad + 1.0], axis=1)
    s1 = jnp.concatenate([-sin, zero, pad], axis=1)
    s2 = jnp.concatenate([zero, sin, pad], axis=1)
    return jnp.stack([jnp.tile(t, (1, reps)) for t in (c, s1, s2)])


INT_MAX = np.int32(2 ** 31 - 1)
MASKED = -1e30
SUBLANES = 8


def _dsa_kernel(topk, qt_ref, iqt_ref, iwt_ref, k_ref, vt_ref, ik_ref, o_ref,
                key_ref, acc_ref, jb_ref, v_ref, nge_ref):
    t = DSA_T
    qi = pl.program_id(1)
    n_kt = qi + 1
    s_len = k_ref.shape[0]
    qpos = lax.broadcasted_iota(jnp.int32, (1, t), 1)
    limit = qi * t + (qpos // CHUNK + 1) * CHUNK
    krow = lax.broadcasted_iota(jnp.int32, (t, t), 0)

    def score_tile(kt, carry):
        start = pl.multiple_of(kt * t, t)
        ikt = ik_ref[pl.ds(start, t), :]
        score = jnp.zeros((t, t), F32)
        for h in range(IDX_HEADS):
            logit = jnp.dot(ikt, iqt_ref[h * IDX_K:(h + 1) * IDX_K, :], preferred_element_type=F32)
            score = score + jnp.maximum(logit, 0.0) * iwt_ref[h:h + 1, :]
        bits = pltpu.bitcast(score, jnp.int32)
        key = bits ^ ((bits >> 31) & jnp.int32(0x7FFFFFFF))
        key_ref[kt] = jnp.where(krow + start < limit, key, INT_MIN)
        return carry

    lax.fori_loop(0, n_kt, score_tile, 0)

    def count(pred, n=None):
        def body(kt, acc):
            ones = jnp.where(pred(kt), 1.0, 0.0)
            return acc + jnp.sum(ones.reshape(t // SUBLANES, SUBLANES, t), axis=0)
        acc = jnp.zeros((SUBLANES, t), F32)
        if n is None:
            acc = lax.fori_loop(0, n_kt, body, acc)
        else:
            for kt in range(n):
                acc = body(kt, acc)
        return jnp.sum(acc, axis=0, keepdims=True)

    for n in range(1, key_ref.shape[0] + 1):
        @pl.when(n_kt == n)
        def _(n=n):
            def vbody(it, carry):
                lo, n_lo = carry
                cand = lo + lax.shift_left(jnp.int32(1), 31 - it)
                cnt = count(lambda kt: key_ref[kt] >= cand, n)
                ok = cnt >= topk
                return jnp.where(ok, cand, lo), jnp.where(ok, cnt, n_lo)

            lo, n_lo = lax.fori_loop(0, 32, vbody, (jnp.full((1, t), INT_MIN, jnp.int32),
                                                    jnp.full((1, t), float(n * t), F32)))
            v_ref[...] = jnp.broadcast_to(lo, v_ref.shape)
            nge_ref[...] = jnp.broadcast_to(n_lo, nge_ref.shape)

    v = v_ref[0:1, :]
    n_ge = nge_ref[0:1, :]

    jb_ref[...] = jnp.full(jb_ref.shape, INT_MAX, jnp.int32)

    @pl.when(jnp.max(n_ge) > topk)
    def _():
        need = topk - count(lambda kt: key_ref[kt] > v)
        nbits = (s_len - 1).bit_length()

        def jbody(it, jb):
            cand = jb + lax.shift_left(jnp.int32(1), nbits - 1 - it)
            few = count(lambda kt: (key_ref[kt] == v) & (krow + kt * t < cand)) < need
            return jnp.where(few, cand, jb)

        jb = lax.fori_loop(0, nbits, jbody, jnp.zeros((1, t), jnp.int32))
        jb_ref[...] = jnp.broadcast_to(jb, jb_ref.shape)

    jb = jnp.where(v == INT_MIN, -1, jb_ref[0:1, :])

    acc_ref[...] = jnp.zeros_like(acc_ref)

    def att_tile(kt, carry):
        ms, ls = carry
        start = pl.multiple_of(kt * t, t)
        key = key_ref[kt]
        tie_bias = jnp.where(key == v, jnp.where(krow + start <= jb, 0.0, MASKED), MASKED)
        bias = jnp.where(key > v, 0.0, tie_bias)
        heads = [slice(h * HEAD_DIM, (h + 1) * HEAD_DIM) for h in range(N_HEADS_A)]
        ss = [jnp.dot(k_ref[pl.ds(start, t), hs], qt_ref[hs, :], preferred_element_type=F32) + bias
              for hs in heads]
        new_ms = [jnp.maximum(ms[h], jnp.max(ss[h], axis=0, keepdims=True)) for h in range(N_HEADS_A)]
        ps = [jnp.exp(ss[h] - new_ms[h]) for h in range(N_HEADS_A)]
        alphas = [jnp.exp(ms[h] - new_ms[h]) for h in range(N_HEADS_A)]
        new_ls = [alphas[h] * ls[h] + jnp.sum(ps[h], axis=0, keepdims=True) for h in range(N_HEADS_A)]
        pvs = [jnp.dot(vt_ref[kt, hs, :], ps[h].astype(BF16), preferred_element_type=F32)
               for h, hs in enumerate(heads)]
        for h in range(N_HEADS_A):
            acc_ref[h] = alphas[h] * acc_ref[h] + pvs[h]
        return tuple(new_ms), tuple(new_ls)

    init = (tuple(jnp.full((1, t), MASKED, F32) for _ in range(N_HEADS_A)),
            tuple(jnp.zeros((1, t), F32) for _ in range(N_HEADS_A)))
    _, ls = lax.fori_loop(0, n_kt, att_tile, init)
    for h in range(N_HEADS_A):
        o_ref[:, h * HEAD_DIM:(h + 1) * HEAD_DIM] = (acc_ref[h] / ls[h]).T.astype(o_ref.dtype)


def _dsa(qt, ka, vt, iqt, ik_ext, iwt):
    bsz, s_len, _ = ka.shape
    t = DSA_T
    n_t = s_len // t
    topk = min(IDX_TOPK_MAX, s_len // 4)

    def qtile(r):
        return pl.BlockSpec((None, None, r, t), lambda b, i: (b, i, 0, 0))

    return pl.pallas_call(
        functools.partial(_dsa_kernel, topk),
        grid=(bsz, n_t),
        in_specs=[qtile(A_WIDTH), qtile(IDX_HEADS * IDX_K), qtile(IDX_HEADS),
                  pl.BlockSpec((None, s_len, A_WIDTH), lambda b, i: (b, 0, 0)),
                  pl.BlockSpec((None, n_t, A_WIDTH, t), lambda b, i: (b, 0, 0, 0)),
                  pl.BlockSpec((None, s_len, IDX_K), lambda b, i: (b, 0, 0))],
        out_specs=pl.BlockSpec((None, t, A_WIDTH), lambda b, i: (b, i, 0)),
        out_shape=jax.ShapeDtypeStruct((bsz, s_len, A_WIDTH), BF16),
        scratch_shapes=[pltpu.VMEM((n_t, t, t), jnp.int32),
                        pltpu.VMEM((N_HEADS_A, HEAD_DIM, t), F32),
                        pltpu.VMEM((SUBLANES, t), jnp.int32),
                        pltpu.VMEM((SUBLANES, t), jnp.int32),
                        pltpu.VMEM((SUBLANES, t), F32)],
        compiler_params=_cparams(("parallel", "arbitrary")),
        name="dsa",
    )(qt, iqt, iwt, ka, vt, ik_ext)


GLA_ROWS = 256


GLA_BLK = SUBLANES
GLA_NBLK = CHUNK // GLA_BLK
GLA_PAIRS = [(i, j) for i in range(GLA_NBLK) for j in range(i)]


def _split3(x):
    hi = x.astype(BF16)
    r1 = x - hi.astype(F32)
    mid = r1.astype(BF16)
    return hi, mid, (r1 - mid.astype(F32)).astype(BF16)


def _gla_cumsum_matrix():
    t = np.arange(CHUNK)
    blk = t // GLA_BLK
    incl = t[None, :] <= t[:, None]
    start = t[None, :] < (GLA_BLK * blk)[:, None]
    end = t[None, :] < (GLA_BLK * (blk + 1))[:, None]
    bounds = t[None, :] < (GLA_BLK * np.arange(GLA_NBLK))[:, None]
    pad = np.zeros((GLA_BLK, CHUNK), bool)
    m = np.concatenate([incl, start, end, bounds, pad], axis=0).astype(np.float32)
    return jnp.asarray(np.concatenate([m, m, m], axis=1), BF16)


def _gla_kernel(q_ref, k_ref, v_ref, og_ref, gl_ref, wg_ref, bg_ref, gn_ref, cm_ref, sg_ref, o_ref,
                st_ref, lg_ref, b_ref, kc_ref):
    @pl.when(pl.program_id(1) == 0)
    def _():
        st_ref[...] = jnp.zeros_like(st_ref)

    g_hi, g_mid, g_lo = (p.astype(F32) for p in _split3(gl_ref[...]))
    r = GLA_GATE_RANK
    packed = (g_hi + pltpu.roll(g_hi, r, 1) + pltpu.roll(g_mid, 2 * r, 1) + pltpu.roll(g_hi, 3 * r, 1)
              + pltpu.roll(g_lo, 4 * r, 1) + pltpu.roll(g_mid, 5 * r, 1))
    gate = jnp.dot(packed.astype(BF16), wg_ref[...], preferred_element_type=F32) + bg_ref[...]
    lg_ref[...] = _log_sigmoid(gate) * (1.0 / GLA_TAU)

    sub = lax.broadcasted_iota(jnp.int32, (GLA_BLK, LANES), 0)
    lane_c = lax.broadcasted_iota(jnp.int32, (GLA_BLK, CHUNK), 1)
    pairs = [slice(p * LANES, (p + 1) * LANES) for p in range(N_HEADS_B * GLA_DK // LANES)]
    first = lax.broadcasted_iota(jnp.int32, (CHUNK, LANES), 1) < GLA_DK

    def chunk(ci, carry):
        rows = pl.ds(ci * CHUNK, CHUNK)
        cums = []
        for ps in pairs:
            parts = _split3(lg_ref[rows, ps])
            cums.append(jnp.dot(cm_ref[...], jnp.concatenate(parts, axis=0), preferred_element_type=F32))
        for p, ps in enumerate(pairs):
            b_ref[rows, ps] = cums[p][0:CHUNK]
        kc_ref[rows, :] = k_ref[rows, :].astype(F32)
        lhss, kbs, xalls, qes, kds, bls = [], [], [], [], [], []
        for p, ps in enumerate(pairs):
            b = cums[p][0:CHUNK]
            b_start = cums[p][CHUNK:2 * CHUNK]
            b_end = cums[p][2 * CHUNK:3 * CHUNK]
            bound = cums[p][3 * CHUNK:3 * CHUNK + GLA_NBLK]
            q = q_ref[rows, ps].astype(F32) * GLA_DK ** -0.5
            k = kc_ref[rows, ps]
            qb = q * jnp.exp(b - b_start)
            kb = k * jnp.exp(b_end - b)
            kbs.append((jnp.where(first, kb, 0.0).astype(BF16), jnp.where(first, 0.0, kb).astype(BF16)))
            dec = [jnp.exp(jnp.where(sub > j, bound - bound[j + 1:j + 2], -jnp.inf))
                   for j in range(GLA_NBLK - 1)]
            lhss.append(jnp.concatenate([qb[i * GLA_BLK:(i + 1) * GLA_BLK] * dec[j][i:i + 1]
                                         for i, j in GLA_PAIRS], axis=0).astype(BF16))
            xs = []
            for s in range(CHUNK):
                blk = slice(s // GLA_BLK * GLA_BLK, (s // GLA_BLK + 1) * GLA_BLK)
                srow = pl.ds(ci * CHUNK + s, 1)
                ks = jnp.broadcast_to(kc_ref[srow, ps], (GLA_BLK, LANES))
                bs = jnp.broadcast_to(b_ref[srow, ps], (GLA_BLK, LANES))
                xs.append(q[blk] * ks * jnp.exp(jnp.where(sub >= s % GLA_BLK, b[blk] - bs, -jnp.inf)))
            xalls.append(jnp.concatenate(xs, axis=0).astype(BF16))
            bl = b[CHUNK - 1:CHUNK]
            kd = k * jnp.exp(bl - b)
            qes += [(q * jnp.exp(b)).astype(BF16)] * 2
            kds += [jnp.where(first, kd, 0.0).astype(BF16), jnp.where(first, 0.0, kd).astype(BF16)]
            bls += [bl, bl]
        cross = [[_dot_t(lhss[p], kbs[p][e]) for e in range(2)] for p in range(len(pairs))]
        sums = [jnp.dot(xalls[p], sg_ref[...], preferred_element_type=F32) for p in range(len(pairs))]
        atts = []
        for p in range(len(pairs)):
            for e in range(2):
                att_rows = []
                for i in range(GLA_NBLK):
                    att = jnp.zeros((GLA_BLK, CHUNK), F32)
                    for g, (gi, j) in enumerate(GLA_PAIRS):
                        if gi == i:
                            att = jnp.where(lane_c // GLA_BLK == j,
                                            cross[p][e][g * GLA_BLK:(g + 1) * GLA_BLK], att)
                    for s in range(i * GLA_BLK, (i + 1) * GLA_BLK):
                        col = sums[p][s * GLA_BLK:(s + 1) * GLA_BLK, e * LANES:e * LANES + CHUNK]
                        att = jnp.where(lane_c == s, col, att)
                    att_rows.append(att)
                atts.append(jnp.concatenate(att_rows, axis=0).astype(BF16))
        for h in range(N_HEADS_B):
            hs = slice(h * GLA_DV, (h + 1) * GLA_DV)
            v = v_ref[rows, hs]
            st = st_ref[h]
            o = _dot_t(qes[h], st.astype(BF16)) + jnp.dot(atts[h], v, preferred_element_type=F32)
            st_ref[h] = st * jnp.exp(bls[h]) + lax.dot_general(
                v, kds[h], (((0,), (0,)), ((), ())), preferred_element_type=F32)
            og = og_ref[rows, hs].astype(F32)
            o_ref[rows, hs] = (_rms(o, gn_ref[...]) * (og * jax.nn.sigmoid(og))).astype(o_ref.dtype)
        return carry

    for ci in range(q_ref.shape[0] // CHUNK):
        chunk(ci, 0)


def _gla(p16, p32, wg, bg, gn):
    bsz, s_len, _ = p16.shape

    def col(c, w):
        return pl.BlockSpec((None, GLA_ROWS, w), lambda b, s, c=c: (b, s, c))

    cm = _gla_cumsum_matrix()
    qk_w = N_HEADS_B * GLA_DK
    lane = np.arange(LANES)[:, None] // GLA_DK
    sg = jnp.asarray(lane == np.arange(2 * LANES)[None, :] // LANES, BF16)
    return pl.pallas_call(
        _gla_kernel,
        grid=(bsz, s_len // GLA_ROWS),
        in_specs=[col(2 * P16_QKB, qk_w), col(2 * P16_QKB + 1, qk_w), col(P16_VB, W512), col(P16_OG, W512),
                  col(P32_GLOW, LANES),
                  pl.BlockSpec((LANES, qk_w), lambda b, s: (0, 0)),
                  pl.BlockSpec((1, qk_w), lambda b, s: (0, 0)),
                  pl.BlockSpec((1, GLA_DV), lambda b, s: (0, 0)),
                  pl.BlockSpec(cm.shape, lambda b, s: (0, 0)),
                  pl.BlockSpec(sg.shape, lambda b, s: (0, 0))],
        out_specs=pl.BlockSpec((None, GLA_ROWS, B_WIDTH), lambda b, s: (b, s, 0)),
        out_shape=jax.ShapeDtypeStruct((bsz, s_len, B_WIDTH), BF16),
        scratch_shapes=[pltpu.VMEM((N_HEADS_B, GLA_DV, LANES), F32),
                        pltpu.VMEM((GLA_ROWS, qk_w), F32),
                        pltpu.VMEM((GLA_ROWS, qk_w), F32),
                        pltpu.VMEM((GLA_ROWS, qk_w), F32)],
        compiler_params=_cparams(("parallel", "arbitrary")),
        name="gla",
    )(p16, p16, p16, p16, p32, wg, bg, gn.reshape(1, GLA_DV), cm, sg)


def _gla_gate_weights(w_gate_up):
    hi, mid, lo = _split3(w_gate_up)
    w = jnp.concatenate([hi, mid, hi, lo, hi, mid], axis=0)
    return jnp.pad(w, ((0, LANES - w.shape[0]), (0, 0)))


SB_T = 256
SB_HEADS = 2
SB_W = SB_HEADS * HEAD_DIM
LOG2E = 1.4426950408889634


def _sb_kernel(q_ref, k_ref, v_ref, gq_ref, gk_ref, cw_ref, o_ref,
               kn_ref, acc_ref, car_ref, zn_ref):
    i = pl.program_id(2)
    heads = [slice(h * HEAD_DIM, (h + 1) * HEAD_DIM) for h in range(SB_HEADS)]

    @pl.when(i == 0)
    def _():
        for hs in heads:
            kn_ref[:, hs] = _rms(k_ref[:, hs].astype(F32), gk_ref[...]).astype(BF16)

    qs = [(_rms(q_ref[:, hs].astype(F32), gq_ref[...]) * (HEAD_DIM ** -0.5 * LOG2E)).astype(BF16)
          for hs in heads]

    def logits(j):
        start = pl.multiple_of(j * SB_T, SB_T)
        return [_dot_t(qs[h], kn_ref[pl.ds(start, SB_T), hs]) for h, hs in enumerate(heads)]

    def block(j, zs, next_slot, diag):
        start = pl.multiple_of(j * SB_T, SB_T)
        if diag:
            strict = (lax.broadcasted_iota(jnp.int32, (SB_T, SB_T), 1)
                      < lax.broadcasted_iota(jnp.int32, (SB_T, SB_T), 0))
        for h, z in enumerate(logits(jnp.maximum(j - 1, 0))):
            zn_ref[next_slot * SB_HEADS + h] = z
        hls = []
        for z in zs:
            sp = jnp.maximum(z, 0.0) + jnp.log2(1.0 + jnp.exp2(-jnp.abs(z)))
            if diag:
                sp = jnp.where(strict, sp, 0.0)
            hls.append(sp.astype(BF16))
        rs = [jnp.dot(hl, cw_ref[...], preferred_element_type=F32) for hl in hls]
        avs = []
        for h in range(SB_HEADS):
            car = car_ref[h]
            a = jnp.exp2(zs[h] - rs[h] - jnp.concatenate([car] * (SB_T // LANES), axis=1))
            if diag:
                a = jnp.where(strict, a, 0.0)
            avs.append(a.astype(BF16))
            car_ref[h] = car + jnp.broadcast_to(rs[h][:, 0:1], car.shape)
        for h, hs in enumerate(heads):
            acc_ref[:, hs] += jnp.dot(avs[h], v_ref[pl.ds(start, SB_T), hs], preferred_element_type=F32)

    acc_ref[...] = jnp.zeros_like(acc_ref)
    car_ref[...] = jnp.zeros_like(car_ref)
    block(i, logits(i), 0, True)

    def trip(j, slot):
        zs = [zn_ref[slot * SB_HEADS + h] for h in range(SB_HEADS)]
        block(j, zs, 1 - slot, False)

    def body(u, carry):
        j = i - 1 - 2 * u
        trip(j, 0)
        trip(j - 1, 1)
        return carry

    lax.fori_loop(0, i // 2, body, 0)

    @pl.when(i % 2 == 1)
    def _():
        trip(0, 0)

    o_ref[...] = acc_ref[...].astype(o_ref.dtype)


def _stick_breaking(qkv, gq, gk):
    bsz, s_len, _ = qkv.shape
    g_spec = pl.BlockSpec((1, HEAD_DIM), lambda b, h, i: (0, 0))
    cum_w = (jnp.arange(SB_T)[:, None] >= jnp.arange(SB_T)[None, :]).astype(BF16)
    n_grp = N_HEADS_C // SB_HEADS
    return pl.pallas_call(
        _sb_kernel,
        grid=(bsz, n_grp, s_len // SB_T),
        in_specs=[pl.BlockSpec((None, SB_T, SB_W), lambda b, h, i: (b, i, h)),
                  pl.BlockSpec((None, s_len, SB_W), lambda b, h, i: (b, 0, n_grp + h)),
                  pl.BlockSpec((None, s_len, SB_W), lambda b, h, i: (b, 0, 2 * n_grp + h)),
                  g_spec, g_spec,
                  pl.BlockSpec((SB_T, SB_T), lambda b, h, i: (0, 0))],
        out_specs=pl.BlockSpec((None, SB_T, SB_W), lambda b, h, i: (b, i, h)),
        out_shape=jax.ShapeDtypeStruct((bsz, s_len, C_WIDTH), BF16),
        scratch_shapes=[pltpu.VMEM((s_len, SB_W), BF16),
                        pltpu.VMEM((SB_T, SB_W), F32), pltpu.VMEM((SB_HEADS, SB_T, LANES), F32),
                        pltpu.VMEM((2 * SB_HEADS, SB_T, SB_T), F32)],
        compiler_params=_cparams(("parallel", "parallel", "arbitrary")),
        name="stick_breaking",
    )(qkv, qkv, qkv, gq.reshape(1, HEAD_DIM), gk.reshape(1, HEAD_DIM), cum_w)


def _pad_heads(w, n_heads, width):
    lead = w.shape[:-1]
    d = w.shape[-1] // n_heads
    w = w.reshape(*lead, n_heads, d)
    w = jnp.pad(w, [(0, 0)] * len(lead) + [(0, 0), (0, width - d)])
    return w.reshape(*lead, n_heads * width)


def _pad_cols(w, width):
    return jnp.pad(w, [(0, 0)] * (w.ndim - 1) + [(0, width - w.shape[-1])])


def _layer0_weights(w_in):
    parts, o = [], 0
    for s in AB_SIZES:
        parts.append(w_in[:, o:o + s])
        o += s
    qa, ka, va, iq, ik, iw, qb, kb, vb, glow, og = parts
    cols = [qa, ka, va, qb, kb, vb, og,
            iq, _pad_cols(ik, LANES), _pad_cols(iw, LANES), _pad_cols(glow, 2 * LANES)]
    return jnp.concatenate(cols, axis=1).astype(BF16)


def kernel(x, g_mix, g_ffn, w_in_ab, gq_a, gk_a, w_gate_up, b_gate, g_gla, w_out_ab,
           w_in_c, gq_c, gk_c, w_out_c, w_up, w_down):
    bsz, s_len, d = x.shape
    m = bsz * s_len
    h = x.reshape(m, d)

    p16, p32 = _norm_matmul(h, g_mix[0], _layer0_weights(w_in_ab[0]), tm=PROJ_TM, tn=W512, n16=N_P16)
    p16 = p16.reshape(bsz, s_len, N_P16 * W512)
    p32 = p32.reshape(bsz, s_len, N_P32 * W512)
    tab_a = _rope_tables(s_len, HEAD_DIM, 1)
    tab_i = _rope_tables(s_len, IDX_DIM, LANES // IDX_DIM)
    oa = _dsa(*_prep_a(p16, p32, gq_a[0], gk_a[0], tab_a, tab_i))
    wg = _gla_gate_weights(w_gate_up[0])
    bg = b_gate[0].reshape(1, N_HEADS_B * GLA_DK)
    ob = _gla(p16, p32, wg, bg, g_gla[0])
    w_o = w_out_ab[0].astype(BF16)
    h = _proj_mlp(h, [oa.reshape(m, A_WIDTH), ob.reshape(m, B_WIDTH)], [w_o[:A_WIDTH], w_o[A_WIDTH:]],
                  g_ffn[0], w_up[0].astype(BF16), w_down[0].astype(BF16), tm=MLP_TM, tf=MLP_TF)

    qkv = _norm_matmul(h, g_mix[1], w_in_c[0].astype(BF16), tm=PROJ_TM, tn=C_WIDTH)
    oc = _stick_breaking(qkv.reshape(bsz, s_len, 3 * C_WIDTH), gq_c[0], gk_c[0])
    h = _proj_mlp(h, [oc.reshape(m, C_WIDTH)], [w_out_c[0].astype(BF16)],
                  g_ffn[1], w_up[1].astype(BF16), w_down[1].astype(BF16), tm=MLP_TM, tf=MLP_TF)
    return h.reshape(bsz, s_len, d)
```

```python
import functools

import numpy as np
import jax
import jax.numpy as jnp
from jax import lax
from jax.experimental import pallas as pl
from jax.experimental.pallas import tpu as pltpu

F32 = jnp.float32
BF16 = jnp.bfloat16

D_MODEL = 1024
CHUNK = 64
HEAD_DIM = 128
EPS = 1e-6
N_HEADS_A = 4
A_WIDTH = N_HEADS_A * HEAD_DIM
IDX_HEADS = 8
IDX_DIM = 64
IDX_TOPK_MAX = 256
N_HEADS_B = 4
GLA_DK = 64
GLA_DV = 128
GLA_GATE_RANK = 16
GLA_TAU = 16.0
B_WIDTH = N_HEADS_B * GLA_DV
N_HEADS_C = 8
C_WIDTH = N_HEADS_C * HEAD_DIM
ROPE_THETA = 500000.0
ROPE_FRACTION = 4
D_FF = 4 * D_MODEL
AB_SIZES = (A_WIDTH, A_WIDTH, A_WIDTH, IDX_HEADS * IDX_DIM, IDX_DIM, IDX_HEADS,
            N_HEADS_B * GLA_DK, N_HEADS_B * GLA_DK, B_WIDTH, GLA_GATE_RANK, B_WIDTH)

LANES = 128
VMEM_LIMIT = 48 * 1024 * 1024
MLP_TM, MLP_TF = 512, 512
PROJ_TM = 512

W512 = 512
P16_QA, P16_KA, P16_VA, P16_QKB, P16_VB, P16_OG = range(6)
N_P16 = 6
P32_IQ = 0
P32_IK, P32_IW, P32_GLOW = 4, 5, 6
N_P32 = 2

INT_MIN = np.int32(-2 ** 31)


def _cparams(sem):
    return pltpu.CompilerParams(dimension_semantics=sem, vmem_limit_bytes=VMEM_LIMIT)


def _rms(x, g):
    return x * lax.rsqrt(jnp.mean(x * x, axis=-1, keepdims=True) + EPS) * g


def _log_sigmoid(x):
    return jnp.minimum(x, 0.0) - jnp.log1p(jnp.exp(-jnp.abs(x)))


def _dot_t(a, b):
    return lax.dot_general(a, b, (((1,), (1,)), ((), ())), preferred_element_type=F32)


def _norm_matmul_kernel(n16, tn, n_qk, x_ref, g_ref, w_ref, hg_ref, *o_refs):
    xn = _rms(x_ref[...], g_ref[...]).astype(BF16)
    for c in range(w_ref.shape[1] // tn):
        y = jnp.dot(xn, w_ref[:, c * tn:(c + 1) * tn], preferred_element_type=F32)
        if c < n_qk:
            y = _head_rms(y, hg_ref[c])
        if c < n16:
            o_refs[0][:, c * tn:(c + 1) * tn] = y.astype(BF16)
        else:
            o_refs[1][:, (c - n16) * tn:(c - n16 + 1) * tn] = y


def _norm_matmul(x, g, w, head_gains, tm, tn, n16=None):
    m, d = x.shape
    n_qk = head_gains.shape[0]
    n = w.shape[1]
    split = n16 is not None
    n16 = n16 if split else n // tn
    out_specs = [pl.BlockSpec((tm, n16 * tn), lambda i: (i, 0))]
    out_shape = [jax.ShapeDtypeStruct((m, n16 * tn), BF16)]
    if split:
        out_specs.append(pl.BlockSpec((tm, n - n16 * tn), lambda i: (i, 0)))
        out_shape.append(jax.ShapeDtypeStruct((m, n - n16 * tn), F32))
    out = pl.pallas_call(
        functools.partial(_norm_matmul_kernel, n16, tn, n_qk),
        grid=(m // tm,),
        in_specs=[pl.BlockSpec((tm, d), lambda i: (i, 0)),
                  pl.BlockSpec((1, d), lambda i: (0, 0)),
                  pl.BlockSpec((d, n), lambda i: (0, 0)),
                  pl.BlockSpec((n_qk, 1, HEAD_DIM), lambda i: (0, 0, 0))],
        out_specs=out_specs,
        out_shape=out_shape,
        compiler_params=_cparams(("parallel",)),
        name="norm_matmul",
    )(x, g.reshape(1, d), w, head_gains.reshape(n_qk, 1, HEAD_DIM))
    return tuple(out) if split else out[0]


def _proj_mlp_kernel(n_a, tf, h_ref, *refs):
    a_refs = refs[:n_a]
    wo_refs = refs[n_a:2 * n_a]
    g_ref, wu_ref, wd_ref, o_ref, xn_ref = refs[2 * n_a:]
    h1 = h_ref[...]
    for a_ref, wo_ref in zip(a_refs, wo_refs):
        h1 = h1 + jnp.dot(a_ref[...], wo_ref[...], preferred_element_type=F32)
    o_ref[...] = h1
    xn_ref[...] = _rms(h1, g_ref[...]).astype(BF16)
    for c in range(wu_ref.shape[1] // tf):
        u = jnp.dot(xn_ref[...], wu_ref[:, c * tf:(c + 1) * tf], preferred_element_type=F32)
        u = jnp.square(jnp.maximum(u, 0.0)).astype(BF16)
        o_ref[...] += jnp.dot(u, wd_ref[c * tf:(c + 1) * tf, :], preferred_element_type=F32)


def _proj_mlp(h, a_list, wo_list, g, w_up, w_down, tm, tf):
    m, d = h.shape
    n_a = len(a_list)

    def resident(w):
        return pl.BlockSpec(w.shape, lambda i: (0, 0))

    in_specs = [pl.BlockSpec((tm, d), lambda i: (i, 0))]
    in_specs += [pl.BlockSpec((tm, a.shape[1]), lambda i: (i, 0)) for a in a_list]
    in_specs += [resident(w) for w in wo_list]
    in_specs += [pl.BlockSpec((1, d), lambda i: (0, 0)), resident(w_up), resident(w_down)]
    return pl.pallas_call(
        functools.partial(_proj_mlp_kernel, n_a, tf),
        grid=(m // tm,),
        in_specs=in_specs,
        out_specs=pl.BlockSpec((tm, d), lambda i: (i, 0)),
        out_shape=jax.ShapeDtypeStruct((m, d), F32),
        scratch_shapes=[pltpu.VMEM((tm, d), BF16)],
        compiler_params=_cparams(("parallel",)),
        name="proj_mlp",
    )(h, *a_list, *wo_list, g.reshape(1, d), w_up, w_down)


def _rope(x, rot_ref, tab_ref):
    out = []
    for s in range(x.shape[1] // LANES):
        xs = x[:, s * LANES:(s + 1) * LANES]
        if xs.dtype == BF16:
            r = jnp.dot(xs, rot_ref[...], preferred_element_type=F32)
        else:
            hi, lo = _hi_lo(xs)
            r = (jnp.dot(hi, rot_ref[...], preferred_element_type=F32)
                 + jnp.dot(lo, rot_ref[...], preferred_element_type=F32))
        out.append(xs.astype(F32) * tab_ref[0] + r * tab_ref[1])
    return out


def _head_rms(x, g):
    out = []
    for s in range(x.shape[1] // HEAD_DIM):
        out.append(_rms(x[:, s * HEAD_DIM:(s + 1) * HEAD_DIM], g))
    return jnp.concatenate(out, axis=1)


DSA_T = 256
IDX_K = 4 * IDX_DIM


def _hi_lo(x):
    hi = x.astype(BF16)
    return hi, (x - hi.astype(F32)).astype(BF16)


def _prep_a_kernel(qa_ref, ka_ref, va_ref, iq_ref, ik_ref, iw_ref, ra_ref, ri_ref, ta_ref, ti_ref,
                   qt_ref, ko_ref, vt_ref, iqt_ref, iko_ref, iwt_ref):
    q = jnp.concatenate(_rope(qa_ref[...], ra_ref, ta_ref), axis=1)
    qt_ref[...] = q.T.astype(BF16)
    ko_ref[...] = jnp.concatenate(_rope(ka_ref[...], ra_ref, ta_ref), axis=1).astype(BF16)
    vt_ref[...] = va_ref[...].astype(F32).T.astype(BF16)
    iq = jnp.concatenate(_rope(iq_ref[...], ri_ref, ti_ref), axis=1) * IDX_DIM ** -0.5
    hi, lo = _hi_lo(iq.T)
    for h in range(IDX_HEADS):
        rows = slice(h * IDX_DIM, (h + 1) * IDX_DIM)
        for part, val in enumerate((hi, hi, lo, lo)):
            base = h * IDX_K + part * IDX_DIM
            iqt_ref[base:base + IDX_DIM, :] = val[rows, :]
    ik = _rope(ik_ref[...], ri_ref, ti_ref)[0]
    ik_hi, ik_lo = _hi_lo(ik)
    ika = ik_hi.astype(F32) + pltpu.roll(ik_lo.astype(F32), IDX_DIM, 1)
    iko_ref[...] = jnp.concatenate([ika, ika], axis=1).astype(BF16)
    iwt_ref[...] = (iw_ref[...] * IDX_HEADS ** -0.5).T[0:IDX_HEADS, :]


def _prep_a(p16, p32, rope_a, rope_i):
    bsz, s_len, _ = p16.shape
    ts = DSA_T
    n_t = s_len // ts

    def col(c, w):
        return pl.BlockSpec((None, ts, w), lambda s, b, c=c: (b, s, c))

    def rows_out(w):
        return pl.BlockSpec((None, ts, w), lambda s, b: (b, s, 0))

    def tile_out(r):
        return pl.BlockSpec((None, None, r, ts), lambda s, b: (b, s, 0, 0))

    tab_spec = pl.BlockSpec((2, ts, LANES), lambda s, b: (0, s, 0))
    rot_spec = pl.BlockSpec((LANES, LANES), lambda s, b: (0, 0))
    sd = jax.ShapeDtypeStruct
    return pl.pallas_call(
        _prep_a_kernel,
        grid=(n_t, bsz),
        in_specs=[col(P16_QA, W512), col(P16_KA, W512), col(P16_VA, W512), col(P32_IQ, W512),
                  col(P32_IK, LANES), col(P32_IW, LANES), rot_spec, rot_spec, tab_spec, tab_spec],
        out_specs=[tile_out(A_WIDTH), rows_out(A_WIDTH), tile_out(A_WIDTH), tile_out(IDX_HEADS * IDX_K),
                   rows_out(IDX_K), tile_out(IDX_HEADS)],
        out_shape=[sd((bsz, n_t, A_WIDTH, ts), BF16), sd((bsz, s_len, A_WIDTH), BF16),
                   sd((bsz, n_t, A_WIDTH, ts), BF16), sd((bsz, n_t, IDX_HEADS * IDX_K, ts), BF16),
                   sd((bsz, s_len, IDX_K), BF16), sd((bsz, n_t, IDX_HEADS, ts), F32)],
        compiler_params=_cparams(("parallel", "parallel")),
        name="prep_a",
    )(p16, p16, p16, p32, p32, p32, rope_a[0], rope_i[0], rope_a[1], rope_i[1])


def _rope_tables(s_len, period, reps):
    rot_dims = period // ROPE_FRACTION
    half = rot_dims // 2
    inv = jnp.power(ROPE_THETA, -jnp.arange(half, dtype=F32) * 2.0 / rot_dims)
    ang = jnp.arange(s_len).astype(F32)[:, None] * inv[None, :]
    cos, sin = jnp.cos(ang), jnp.sin(ang)
    pad = jnp.zeros((s_len, period - rot_dims), F32)
    c = jnp.concatenate([cos, cos, pad + 1.0], axis=1)
    s = jnp.concatenate([sin, sin, pad], axis=1)
    lane = np.arange(LANES)
    pos = lane % period
    src, dst = lane[:, None], lane[None, :]
    rot = (np.where((src == dst - half) & (pos[None, :] >= half) & (pos[None, :] < rot_dims), 1.0, 0.0)
           - np.where((src == dst + half) & (pos[None, :] < half), 1.0, 0.0))
    return jnp.asarray(rot, BF16), jnp.stack([jnp.tile(t, (1, reps)) for t in (c, s)])


INT_MAX = np.int32(2 ** 31 - 1)
MASKED = -1e30
SUBLANES = 8


def _dsa_kernel(topk, qt_ref, iqt_ref, iwt_ref, k_ref, vt_ref, ik_ref, o_ref,
                key_ref, acc_ref, jb_ref, v_ref, nge_ref):
    t = DSA_T
    qi = pl.program_id(1)
    n_kt = qi + 1
    s_len = k_ref.shape[0]
    qpos = lax.broadcasted_iota(jnp.int32, (1, t), 1)
    limit = qi * t + (qpos // CHUNK + 1) * CHUNK
    krow = lax.broadcasted_iota(jnp.int32, (t, t), 0)

    def score_tile(kt, carry):
        start = pl.multiple_of(kt * t, t)
        ikt = ik_ref[pl.ds(start, t), :]
        score = jnp.zeros((t, t), F32)
        for h in range(IDX_HEADS):
            logit = jnp.dot(ikt, iqt_ref[h * IDX_K:(h + 1) * IDX_K, :], preferred_element_type=F32)
            score = score + jnp.maximum(logit, 0.0) * iwt_ref[h:h + 1, :]
        bits = pltpu.bitcast(score, jnp.int32)
        key = bits ^ ((bits >> 31) & jnp.int32(0x7FFFFFFF))
        key_ref[kt] = jnp.where(krow + start < limit, key, INT_MIN)
        return carry

    lax.fori_loop(0, n_kt, score_tile, 0)

    def count(pred, n=None):
        def body(kt, acc):
            ones = jnp.where(pred(kt), 1.0, 0.0)
            return acc + jnp.sum(ones.reshape(t // SUBLANES, SUBLANES, t), axis=0)
        acc = jnp.zeros((SUBLANES, t), F32)
        if n is None:
            acc = lax.fori_loop(0, n_kt, body, acc)
        else:
            for kt in range(n):
                acc = body(kt, acc)
        return jnp.sum(acc, axis=0, keepdims=True)

    for n in range(1, key_ref.shape[0] + 1):
        @pl.when(n_kt == n)
        def _(n=n):
            def vbody(it, carry):
                lo, n_lo = carry
                cand = lo + lax.shift_left(jnp.int32(1), 31 - it)
                cnt = count(lambda kt: key_ref[kt] >= cand, n)
                ok = cnt >= topk
                return jnp.where(ok, cand, lo), jnp.where(ok, cnt, n_lo)

            lo, n_lo = lax.fori_loop(0, 32, vbody, (jnp.full((1, t), INT_MIN, jnp.int32),
                                                    jnp.full((1, t), float(n * t), F32)))
            v_ref[...] = jnp.broadcast_to(lo, v_ref.shape)
            nge_ref[...] = jnp.broadcast_to(n_lo, nge_ref.shape)

    v = v_ref[0:1, :]
    n_ge = nge_ref[0:1, :]

    jb_ref[...] = jnp.full(jb_ref.shape, INT_MAX, jnp.int32)

    @pl.when(jnp.max(n_ge) > topk)
    def _():
        need = topk - count(lambda kt: key_ref[kt] > v)
        nbits = (s_len - 1).bit_length()

        def jbody(it, jb):
            cand = jb + lax.shift_left(jnp.int32(1), nbits - 1 - it)
            few = count(lambda kt: (key_ref[kt] == v) & (krow + kt * t < cand)) < need
            return jnp.where(few, cand, jb)

        jb = lax.fori_loop(0, nbits, jbody, jnp.zeros((1, t), jnp.int32))
        jb_ref[...] = jnp.broadcast_to(jb, jb_ref.shape)

    jb = jnp.where(v == INT_MIN, -1, jb_ref[0:1, :])

    acc_ref[...] = jnp.zeros_like(acc_ref)

    def att_tile(kt, carry):
        ms, ls = carry
        start = pl.multiple_of(kt * t, t)
        key = key_ref[kt]
        tie_bias = jnp.where(key == v, jnp.where(krow + start <= jb, 0.0, MASKED), MASKED)
        bias = jnp.where(key > v, 0.0, tie_bias)
        heads = [slice(h * HEAD_DIM, (h + 1) * HEAD_DIM) for h in range(N_HEADS_A)]
        ss = [jnp.dot(k_ref[pl.ds(start, t), hs], qt_ref[hs, :], preferred_element_type=F32) + bias
              for hs in heads]
        new_ms = [jnp.maximum(ms[h], jnp.max(ss[h], axis=0, keepdims=True)) for h in range(N_HEADS_A)]
        ps = [jnp.exp(ss[h] - new_ms[h]) for h in range(N_HEADS_A)]
        alphas = [jnp.exp(ms[h] - new_ms[h]) for h in range(N_HEADS_A)]
        new_ls = [alphas[h] * ls[h] + jnp.sum(ps[h], axis=0, keepdims=True) for h in range(N_HEADS_A)]
        pvs = [jnp.dot(vt_ref[kt, hs, :], ps[h].astype(BF16), preferred_element_type=F32)
               for h, hs in enumerate(heads)]
        for h in range(N_HEADS_A):
            acc_ref[h] = alphas[h] * acc_ref[h] + pvs[h]
        return tuple(new_ms), tuple(new_ls)

    init = (tuple(jnp.full((1, t), MASKED, F32) for _ in range(N_HEADS_A)),
            tuple(jnp.zeros((1, t), F32) for _ in range(N_HEADS_A)))
    _, ls = lax.fori_loop(0, n_kt, att_tile, init)
    for h in range(N_HEADS_A):
        o_ref[:, h * HEAD_DIM:(h + 1) * HEAD_DIM] = (acc_ref[h] / ls[h]).T.astype(o_ref.dtype)


def _dsa(qt, ka, vt, iqt, ik_ext, iwt):
    bsz, s_len, _ = ka.shape
    t = DSA_T
    n_t = s_len // t
    topk = min(IDX_TOPK_MAX, s_len // 4)

    def qtile(r):
        return pl.BlockSpec((None, None, r, t), lambda b, i: (b, i, 0, 0))

    return pl.pallas_call(
        functools.partial(_dsa_kernel, topk),
        grid=(bsz, n_t),
        in_specs=[qtile(A_WIDTH), qtile(IDX_HEADS * IDX_K), qtile(IDX_HEADS),
                  pl.BlockSpec((None, s_len, A_WIDTH), lambda b, i: (b, 0, 0)),
                  pl.BlockSpec((None, n_t, A_WIDTH, t), lambda b, i: (b, 0, 0, 0)),
                  pl.BlockSpec((None, s_len, IDX_K), lambda b, i: (b, 0, 0))],
        out_specs=pl.BlockSpec((None, t, A_WIDTH), lambda b, i: (b, i, 0)),
        out_shape=jax.ShapeDtypeStruct((bsz, s_len, A_WIDTH), BF16),
        scratch_shapes=[pltpu.VMEM((n_t, t, t), jnp.int32),
                        pltpu.VMEM((N_HEADS_A, HEAD_DIM, t), F32),
                        pltpu.VMEM((SUBLANES, t), jnp.int32),
                        pltpu.VMEM((SUBLANES, t), jnp.int32),
                        pltpu.VMEM((SUBLANES, t), F32)],
        compiler_params=_cparams(("parallel", "arbitrary")),
        name="dsa",
    )(qt, iqt, iwt, ka, vt, ik_ext)


GLA_ROWS = 256


GLA_BLK = SUBLANES
GLA_NBLK = CHUNK // GLA_BLK
GLA_PAIRS = [(i, j) for i in range(GLA_NBLK) for j in range(i)]


def _split3(x):
    hi = x.astype(BF16)
    r1 = x - hi.astype(F32)
    mid = r1.astype(BF16)
    return hi, mid, (r1 - mid.astype(F32)).astype(BF16)


def _gla_cumsum_matrix():
    t = np.arange(CHUNK)
    blk = t // GLA_BLK
    incl = t[None, :] <= t[:, None]
    start = t[None, :] < (GLA_BLK * blk)[:, None]
    end = t[None, :] < (GLA_BLK * (blk + 1))[:, None]
    bounds = t[None, :] < (GLA_BLK * np.arange(GLA_NBLK))[:, None]
    pad = np.zeros((GLA_BLK, CHUNK), bool)
    m = np.concatenate([incl, start, end, bounds, pad], axis=0).astype(np.float32)
    return jnp.asarray(np.concatenate([m, m, m], axis=1), BF16)


def _gla_kernel(q_ref, k_ref, v_ref, og_ref, gl_ref, wg_ref, bg_ref, gn_ref, cm_ref, sg_ref, o_ref,
                st_ref, lg_ref, b_ref, kc_ref):
    @pl.when(pl.program_id(1) == 0)
    def _():
        st_ref[...] = jnp.zeros_like(st_ref)

    g_hi, g_mid, g_lo = (p.astype(F32) for p in _split3(gl_ref[...]))
    r = GLA_GATE_RANK
    packed = (g_hi + pltpu.roll(g_hi, r, 1) + pltpu.roll(g_mid, 2 * r, 1) + pltpu.roll(g_hi, 3 * r, 1)
              + pltpu.roll(g_lo, 4 * r, 1) + pltpu.roll(g_mid, 5 * r, 1))
    gate = jnp.dot(packed.astype(BF16), wg_ref[...], preferred_element_type=F32) + bg_ref[...]
    lg_ref[...] = _log_sigmoid(gate) * (1.0 / GLA_TAU)

    sub = lax.broadcasted_iota(jnp.int32, (GLA_BLK, LANES), 0)
    lane_c = lax.broadcasted_iota(jnp.int32, (GLA_BLK, CHUNK), 1)
    pairs = [slice(p * LANES, (p + 1) * LANES) for p in range(N_HEADS_B * GLA_DK // LANES)]
    first = lax.broadcasted_iota(jnp.int32, (CHUNK, LANES), 1) < GLA_DK

    def chunk(ci, carry):
        rows = pl.ds(ci * CHUNK, CHUNK)
        cums = []
        for ps in pairs:
            parts = _split3(lg_ref[rows, ps])
            cums.append(jnp.dot(cm_ref[...], jnp.concatenate(parts, axis=0), preferred_element_type=F32))
        for p, ps in enumerate(pairs):
            b_ref[rows, ps] = cums[p][0:CHUNK]
        kc_ref[rows, :] = k_ref[rows, :].astype(F32)
        lhss, kbs, xalls, qes, kds, bls = [], [], [], [], [], []
        for p, ps in enumerate(pairs):
            b = cums[p][0:CHUNK]
            b_start = cums[p][CHUNK:2 * CHUNK]
            b_end = cums[p][2 * CHUNK:3 * CHUNK]
            bound = cums[p][3 * CHUNK:3 * CHUNK + GLA_NBLK]
            q = q_ref[rows, ps].astype(F32) * GLA_DK ** -0.5
            k = kc_ref[rows, ps]
            qb = q * jnp.exp(b - b_start)
            kb = k * jnp.exp(b_end - b)
            kbs.append((jnp.where(first, kb, 0.0).astype(BF16), jnp.where(first, 0.0, kb).astype(BF16)))
            dec = [jnp.exp(jnp.where(sub > j, bound - bound[j + 1:j + 2], -jnp.inf))
                   for j in range(GLA_NBLK - 1)]
            lhss.append(jnp.concatenate([qb[i * GLA_BLK:(i + 1) * GLA_BLK] * dec[j][i:i + 1]
                                         for i, j in GLA_PAIRS], axis=0).astype(BF16))
            xs = []
            for s in range(CHUNK):
                blk = slice(s // GLA_BLK * GLA_BLK, (s // GLA_BLK + 1) * GLA_BLK)
                srow = pl.ds(ci * CHUNK + s, 1)
                ks = jnp.broadcast_to(kc_ref[srow, ps], (GLA_BLK, LANES))
                bs = jnp.broadcast_to(b_ref[srow, ps], (GLA_BLK, LANES))
                xs.append(q[blk] * ks * jnp.exp(jnp.where(sub >= s % GLA_BLK, b[blk] - bs, -jnp.inf)))
            xalls.append(jnp.concatenate(xs, axis=0).astype(BF16))
            bl = b[CHUNK - 1:CHUNK]
            kd = k * jnp.exp(bl - b)
            qes += [(q * jnp.exp(b)).astype(BF16)] * 2
            kds += [jnp.where(first, kd, 0.0).astype(BF16), jnp.where(first, 0.0, kd).astype(BF16)]
            bls += [bl, bl]
        cross = [[_dot_t(lhss[p], kbs[p][e]) for e in range(2)] for p in range(len(pairs))]
        sums = [jnp.dot(xalls[p], sg_ref[...], preferred_element_type=F32) for p in range(len(pairs))]
        atts = []
        for p in range(len(pairs)):
            for e in range(2):
                att_rows = []
                for i in range(GLA_NBLK):
                    att = jnp.zeros((GLA_BLK, CHUNK), F32)
                    for g, (gi, j) in enumerate(GLA_PAIRS):
                        if gi == i:
                            att = jnp.where(lane_c // GLA_BLK == j,
                                            cross[p][e][g * GLA_BLK:(g + 1) * GLA_BLK], att)
                    for s in range(i * GLA_BLK, (i + 1) * GLA_BLK):
                        col = sums[p][s * GLA_BLK:(s + 1) * GLA_BLK, e * LANES:e * LANES + CHUNK]
                        att = jnp.where(lane_c == s, col, att)
                    att_rows.append(att)
                atts.append(jnp.concatenate(att_rows, axis=0).astype(BF16))
        for h in range(N_HEADS_B):
            hs = slice(h * GLA_DV, (h + 1) * GLA_DV)
            v = v_ref[rows, hs]
            st = st_ref[h]
            o = _dot_t(qes[h], st.astype(BF16)) + jnp.dot(atts[h], v, preferred_element_type=F32)
            st_ref[h] = st * jnp.exp(bls[h]) + lax.dot_general(
                v, kds[h], (((0,), (0,)), ((), ())), preferred_element_type=F32)
            og = og_ref[rows, hs].astype(F32)
            o_ref[rows, hs] = (_rms(o, gn_ref[...]) * (og * jax.nn.sigmoid(og))).astype(o_ref.dtype)
        return carry

    for ci in range(q_ref.shape[0] // CHUNK):
        chunk(ci, 0)


def _gla(p16, p32, wg, bg, gn):
    bsz, s_len, _ = p16.shape

    def col(c, w):
        return pl.BlockSpec((None, GLA_ROWS, w), lambda b, s, c=c: (b, s, c))

    cm = _gla_cumsum_matrix()
    qk_w = N_HEADS_B * GLA_DK
    lane = np.arange(LANES)[:, None] // GLA_DK
    sg = jnp.asarray(lane == np.arange(2 * LANES)[None, :] // LANES, BF16)
    return pl.pallas_call(
        _gla_kernel,
        grid=(bsz, s_len // GLA_ROWS),
        in_specs=[col(2 * P16_QKB, qk_w), col(2 * P16_QKB + 1, qk_w), col(P16_VB, W512), col(P16_OG, W512),
                  col(P32_GLOW, LANES),
                  pl.BlockSpec((LANES, qk_w), lambda b, s: (0, 0)),
                  pl.BlockSpec((1, qk_w), lambda b, s: (0, 0)),
                  pl.BlockSpec((1, GLA_DV), lambda b, s: (0, 0)),
                  pl.BlockSpec(cm.shape, lambda b, s: (0, 0)),
                  pl.BlockSpec(sg.shape, lambda b, s: (0, 0))],
        out_specs=pl.BlockSpec((None, GLA_ROWS, B_WIDTH), lambda b, s: (b, s, 0)),
        out_shape=jax.ShapeDtypeStruct((bsz, s_len, B_WIDTH), BF16),
        scratch_shapes=[pltpu.VMEM((N_HEADS_B, GLA_DV, LANES), F32),
                        pltpu.VMEM((GLA_ROWS, qk_w), F32),
                        pltpu.VMEM((GLA_ROWS, qk_w), F32),
                        pltpu.VMEM((GLA_ROWS, qk_w), F32)],
        compiler_params=_cparams(("parallel", "arbitrary")),
        name="gla",
    )(p16, p16, p16, p16, p32, wg, bg, gn.reshape(1, GLA_DV), cm, sg)


def _gla_gate_weights(w_gate_up):
    hi, mid, lo = _split3(w_gate_up)
    w = jnp.concatenate([hi, mid, hi, lo, hi, mid], axis=0)
    return jnp.pad(w, ((0, LANES - w.shape[0]), (0, 0)))


SB_T = 256
SB_HEADS = 2
SB_W = SB_HEADS * HEAD_DIM
LOG2E = 1.4426950408889634


def _sb_kernel(q_ref, kn_ref, v_ref, cw_ref, o_ref, acc_ref, car_ref, zn_ref):
    i = pl.program_id(2)
    heads = [slice(h * HEAD_DIM, (h + 1) * HEAD_DIM) for h in range(SB_HEADS)]
    qs = [q_ref[:, hs] for hs in heads]

    def logits(j):
        start = pl.multiple_of(j * SB_T, SB_T)
        return [_dot_t(qs[h], kn_ref[pl.ds(start, SB_T), hs]) for h, hs in enumerate(heads)]

    def block(j, zs, next_slot, diag):
        start = pl.multiple_of(j * SB_T, SB_T)
        if diag:
            strict = (lax.broadcasted_iota(jnp.int32, (SB_T, SB_T), 1)
                      < lax.broadcasted_iota(jnp.int32, (SB_T, SB_T), 0))
        for h, z in enumerate(logits(jnp.maximum(j - 1, 0))):
            zn_ref[next_slot * SB_HEADS + h] = z
        hls = []
        for z in zs:
            sp = jnp.maximum(z, 0.0) + jnp.log2(1.0 + jnp.exp2(-jnp.abs(z)))
            if diag:
                sp = jnp.where(strict, sp, 0.0)
            hls.append(sp.astype(BF16))
        rs = [jnp.dot(hl, cw_ref[...], preferred_element_type=F32) for hl in hls]
        avs = []
        for h in range(SB_HEADS):
            car = car_ref[h]
            a = jnp.exp2(zs[h] - rs[h] - jnp.concatenate([car] * (SB_T // LANES), axis=1))
            if diag:
                a = jnp.where(strict, a, 0.0)
            avs.append(a.astype(BF16))
            car_ref[h] = car + jnp.broadcast_to(rs[h][:, 0:1], car.shape)
        for h, hs in enumerate(heads):
            acc_ref[:, hs] += jnp.dot(avs[h], v_ref[pl.ds(start, SB_T), hs], preferred_element_type=F32)

    acc_ref[...] = jnp.zeros_like(acc_ref)
    car_ref[...] = jnp.zeros_like(car_ref)

    def trip(j, slot):
        zs = [zn_ref[slot * SB_HEADS + h] for h in range(SB_HEADS)]
        block(j, zs, 1 - slot, False)

    odd = i % 2

    @pl.when(odd == 0)
    def _():
        block(i, logits(i), 0, True)

    @pl.when(odd == 1)
    def _():
        block(i, logits(i), 1, True)
        trip(i - 1, 1)

    def body(u, carry):
        j = i - 1 - odd - 2 * u
        trip(j, 0)
        trip(j - 1, 1)
        return carry

    lax.fori_loop(0, i // 2, body, 0)
    o_ref[...] = acc_ref[...].astype(o_ref.dtype)


def _stick_breaking(qkv):
    bsz, s_len, _ = qkv.shape
    cum_w = (jnp.arange(SB_T)[:, None] >= jnp.arange(SB_T)[None, :]).astype(BF16)
    n_grp = N_HEADS_C // SB_HEADS
    return pl.pallas_call(
        _sb_kernel,
        grid=(bsz, n_grp, s_len // SB_T),
        in_specs=[pl.BlockSpec((None, SB_T, SB_W), lambda b, h, i: (b, i, h)),
                  pl.BlockSpec((None, s_len, SB_W), lambda b, h, i: (b, 0, n_grp + h)),
                  pl.BlockSpec((None, s_len, SB_W), lambda b, h, i: (b, 0, 2 * n_grp + h)),
                  pl.BlockSpec((SB_T, SB_T), lambda b, h, i: (0, 0))],
        out_specs=pl.BlockSpec((None, SB_T, SB_W), lambda b, h, i: (b, i, h)),
        out_shape=jax.ShapeDtypeStruct((bsz, s_len, C_WIDTH), BF16),
        scratch_shapes=[pltpu.VMEM((SB_T, SB_W), F32), pltpu.VMEM((SB_HEADS, SB_T, LANES), F32),
                        pltpu.VMEM((2 * SB_HEADS, SB_T, SB_T), F32)],
        compiler_params=_cparams(("parallel", "parallel", "arbitrary")),
        name="stick_breaking",
    )(qkv, qkv, qkv, cum_w)


def _pad_heads(w, n_heads, width):
    lead = w.shape[:-1]
    d = w.shape[-1] // n_heads
    w = w.reshape(*lead, n_heads, d)
    w = jnp.pad(w, [(0, 0)] * len(lead) + [(0, 0), (0, width - d)])
    return w.reshape(*lead, n_heads * width)


def _pad_cols(w, width):
    return jnp.pad(w, [(0, 0)] * (w.ndim - 1) + [(0, width - w.shape[-1])])


def _layer0_weights(w_in):
    parts, o = [], 0
    for s in AB_SIZES:
        parts.append(w_in[:, o:o + s])
        o += s
    qa, ka, va, iq, ik, iw, qb, kb, vb, glow, og = parts
    cols = [qa, ka, va, qb, kb, vb, og,
            iq, _pad_cols(ik, LANES), _pad_cols(iw, LANES), _pad_cols(glow, 2 * LANES)]
    return jnp.concatenate(cols, axis=1).astype(BF16)


def kernel(x, g_mix, g_ffn, w_in_ab, gq_a, gk_a, w_gate_up, b_gate, g_gla, w_out_ab,
           w_in_c, gq_c, gk_c, w_out_c, w_up, w_down):
    bsz, s_len, d = x.shape
    m = bsz * s_len
    h = x.reshape(m, d)

    qk_gains = jnp.stack([gq_a[0] * HEAD_DIM ** -0.5, gk_a[0]])
    p16, p32 = _norm_matmul(h, g_mix[0], _layer0_weights(w_in_ab[0]), qk_gains, tm=PROJ_TM, tn=W512, n16=N_P16)
    p16 = p16.reshape(bsz, s_len, N_P16 * W512)
    p32 = p32.reshape(bsz, s_len, N_P32 * W512)
    rope_a = _rope_tables(s_len, HEAD_DIM, 1)
    rope_i = _rope_tables(s_len, IDX_DIM, LANES // IDX_DIM)
    oa = _dsa(*_prep_a(p16, p32, rope_a, rope_i))
    wg = _gla_gate_weights(w_gate_up[0])
    bg = b_gate[0].reshape(1, N_HEADS_B * GLA_DK)
    ob = _gla(p16, p32, wg, bg, g_gla[0])
    w_o = w_out_ab[0].astype(BF16)
    h = _proj_mlp(h, [oa.reshape(m, A_WIDTH), ob.reshape(m, B_WIDTH)], [w_o[:A_WIDTH], w_o[A_WIDTH:]],
                  g_ffn[0], w_up[0].astype(BF16), w_down[0].astype(BF16), tm=MLP_TM, tf=MLP_TF)

    qk_gains = jnp.stack([gq_c[0] * (HEAD_DIM ** -0.5 * LOG2E), gk_c[0]])
    qkv = _norm_matmul(h, g_mix[1], w_in_c[0].astype(BF16), qk_gains, tm=PROJ_TM, tn=C_WIDTH)
    oc = _stick_breaking(qkv.reshape(bsz, s_len, 3 * C_WIDTH))
    h = _proj_mlp(h, [oc.reshape(m, C_WIDTH)], [w_out_c[0].astype(BF16)],
                  g_ffn[1], w_up[1].astype(BF16), w_down[1].astype(BF16), tm=MLP_TM, tf=MLP_TF)
    return h.reshape(bsz, s_len, d)
```

```python
import functools

import numpy as np
import jax
import jax.numpy as jnp
from jax import lax
from jax.experimental import pallas as pl
from jax.experimental.pallas import tpu as pltpu

F32 = jnp.float32
BF16 = jnp.bfloat16

D_MODEL = 1024
CHUNK = 64
HEAD_DIM = 128
EPS = 1e-6
N_HEADS_A = 4
A_WIDTH = N_HEADS_A * HEAD_DIM
IDX_HEADS = 8
IDX_DIM = 64
IDX_TOPK_MAX = 256
N_HEADS_B = 4
GLA_DK = 64
GLA_DV = 128
GLA_GATE_RANK = 16
GLA_TAU = 16.0
B_WIDTH = N_HEADS_B * GLA_DV
N_HEADS_C = 8
C_WIDTH = N_HEADS_C * HEAD_DIM
ROPE_THETA = 500000.0
ROPE_FRACTION = 4
D_FF = 4 * D_MODEL
AB_SIZES = (A_WIDTH, A_WIDTH, A_WIDTH, IDX_HEADS * IDX_DIM, IDX_DIM, IDX_HEADS,
            N_HEADS_B * GLA_DK, N_HEADS_B * GLA_DK, B_WIDTH, GLA_GATE_RANK, B_WIDTH)

LANES = 128
VMEM_LIMIT = 48 * 1024 * 1024
MLP_TM, MLP_TF = 512, 512
PROJ_TM = 512

W512 = 512
P16_QA, P16_KA, P16_VA, P16_QKB, P16_VB, P16_OG = range(6)
N_P16 = 6
P32_IQ = 0
P32_IK, P32_IW, P32_GLOW = 4, 5, 6
N_P32 = 2

INT_MIN = np.int32(-2 ** 31)


def _cparams(sem):
    return pltpu.CompilerParams(dimension_semantics=sem, vmem_limit_bytes=VMEM_LIMIT)


def _rms(x, g):
    return x * lax.rsqrt(jnp.mean(x * x, axis=-1, keepdims=True) + EPS) * g


def _log_sigmoid(x):
    return jnp.minimum(x, 0.0) - jnp.log1p(jnp.exp(-jnp.abs(x)))


def _dot_t(a, b):
    return lax.dot_general(a, b, (((1,), (1,)), ((), ())), preferred_element_type=F32)


def _norm_matmul_kernel(n16, tn, n_qk, x_ref, g_ref, w_ref, hg_ref, *o_refs):
    xn = _rms(x_ref[...], g_ref[...]).astype(BF16)
    for c in range(w_ref.shape[1] // tn):
        y = jnp.dot(xn, w_ref[:, c * tn:(c + 1) * tn], preferred_element_type=F32)
        if c < n_qk:
            y = _head_rms(y, hg_ref[c])
        if c < n16:
            o_refs[0][:, c * tn:(c + 1) * tn] = y.astype(BF16)
        else:
            o_refs[1][:, (c - n16) * tn:(c - n16 + 1) * tn] = y


def _norm_matmul(x, g, w, head_gains, tm, tn, n16=None):
    m, d = x.shape
    n_qk = head_gains.shape[0]
    n = w.shape[1]
    split = n16 is not None
    n16 = n16 if split else n // tn
    out_specs = [pl.BlockSpec((tm, n16 * tn), lambda i: (i, 0))]
    out_shape = [jax.ShapeDtypeStruct((m, n16 * tn), BF16)]
    if split:
        out_specs.append(pl.BlockSpec((tm, n - n16 * tn), lambda i: (i, 0)))
        out_shape.append(jax.ShapeDtypeStruct((m, n - n16 * tn), F32))
    out = pl.pallas_call(
        functools.partial(_norm_matmul_kernel, n16, tn, n_qk),
        grid=(m // tm,),
        in_specs=[pl.BlockSpec((tm, d), lambda i: (i, 0)),
                  pl.BlockSpec((1, d), lambda i: (0, 0)),
                  pl.BlockSpec((d, n), lambda i: (0, 0)),
                  pl.BlockSpec((n_qk, 1, HEAD_DIM), lambda i: (0, 0, 0))],
        out_specs=out_specs,
        out_shape=out_shape,
        compiler_params=_cparams(("parallel",)),
        name="norm_matmul",
    )(x, g.reshape(1, d), w, head_gains.reshape(n_qk, 1, HEAD_DIM))
    return tuple(out) if split else out[0]


def _proj_mlp_kernel(n_a, tf, h_ref, *refs):
    a_refs = refs[:n_a]
    wo_refs = refs[n_a:2 * n_a]
    g_ref, wu_ref, wd_ref, o_ref, xn_ref = refs[2 * n_a:]
    h1 = h_ref[...]
    for a_ref, wo_ref in zip(a_refs, wo_refs):
        h1 = h1 + jnp.dot(a_ref[...], wo_ref[...], preferred_element_type=F32)
    o_ref[...] = h1
    xn_ref[...] = _rms(h1, g_ref[...]).astype(BF16)
    for c in range(wu_ref.shape[1] // tf):
        u = jnp.dot(xn_ref[...], wu_ref[:, c * tf:(c + 1) * tf], preferred_element_type=F32)
        u = jnp.square(jnp.maximum(u, 0.0)).astype(BF16)
        o_ref[...] += jnp.dot(u, wd_ref[c * tf:(c + 1) * tf, :], preferred_element_type=F32)


def _proj_mlp(h, a_list, wo_list, g, w_up, w_down, tm, tf):
    m, d = h.shape
    n_a = len(a_list)

    def resident(w):
        return pl.BlockSpec(w.shape, lambda i: (0, 0))

    in_specs = [pl.BlockSpec((tm, d), lambda i: (i, 0))]
    in_specs += [pl.BlockSpec((tm, a.shape[1]), lambda i: (i, 0)) for a in a_list]
    in_specs += [resident(w) for w in wo_list]
    in_specs += [pl.BlockSpec((1, d), lambda i: (0, 0)), resident(w_up), resident(w_down)]
    return pl.pallas_call(
        functools.partial(_proj_mlp_kernel, n_a, tf),
        grid=(m // tm,),
        in_specs=in_specs,
        out_specs=pl.BlockSpec((tm, d), lambda i: (i, 0)),
        out_shape=jax.ShapeDtypeStruct((m, d), F32),
        scratch_shapes=[pltpu.VMEM((tm, d), BF16)],
        compiler_params=_cparams(("parallel",)),
        name="proj_mlp",
    )(h, *a_list, *wo_list, g.reshape(1, d), w_up, w_down)


def _rope(x, rot_ref, tab_ref):
    out = []
    for s in range(x.shape[1] // LANES):
        xs = x[:, s * LANES:(s + 1) * LANES]
        if xs.dtype == BF16:
            r = jnp.dot(xs, rot_ref[...], preferred_element_type=F32)
        else:
            hi, lo = _hi_lo(xs)
            r = (jnp.dot(hi, rot_ref[...], preferred_element_type=F32)
                 + jnp.dot(lo, rot_ref[...], preferred_element_type=F32))
        out.append(xs.astype(F32) * tab_ref[0] + r * tab_ref[1])
    return out


def _head_rms(x, g):
    out = []
    for s in range(x.shape[1] // HEAD_DIM):
        out.append(_rms(x[:, s * HEAD_DIM:(s + 1) * HEAD_DIM], g))
    return jnp.concatenate(out, axis=1)


DSA_T = 256
IDX_K = 4 * IDX_DIM


def _hi_lo(x):
    hi = x.astype(BF16)
    return hi, (x - hi.astype(F32)).astype(BF16)


def _prep_a_kernel(qa_ref, ka_ref, va_ref, iq_ref, ik_ref, iw_ref, ra_ref, ri_ref, ta_ref, ti_ref,
                   qt_ref, ko_ref, vt_ref, iqt_ref, iko_ref, iwt_ref):
    q = jnp.concatenate(_rope(qa_ref[...], ra_ref, ta_ref), axis=1)
    qt_ref[...] = q.T.astype(BF16)
    ko_ref[...] = jnp.concatenate(_rope(ka_ref[...], ra_ref, ta_ref), axis=1).astype(BF16)
    vt_ref[...] = va_ref[...].astype(F32).T.astype(BF16)
    iq = jnp.concatenate(_rope(iq_ref[...], ri_ref, ti_ref), axis=1) * IDX_DIM ** -0.5
    hi, lo = _hi_lo(iq.T)
    for h in range(IDX_HEADS):
        rows = slice(h * IDX_DIM, (h + 1) * IDX_DIM)
        for part, val in enumerate((hi, hi, lo, lo)):
            base = h * IDX_K + part * IDX_DIM
            iqt_ref[base:base + IDX_DIM, :] = val[rows, :]
    ik = _rope(ik_ref[...], ri_ref, ti_ref)[0]
    ik_hi, ik_lo = _hi_lo(ik)
    ika = ik_hi.astype(F32) + pltpu.roll(ik_lo.astype(F32), IDX_DIM, 1)
    iko_ref[...] = jnp.concatenate([ika, ika], axis=1).astype(BF16)
    iwt_ref[...] = (iw_ref[...] * IDX_HEADS ** -0.5).T[0:IDX_HEADS, :]


def _prep_a(p16, p32, rope_a, rope_i):
    bsz, s_len, _ = p16.shape
    ts = DSA_T
    n_t = s_len // ts

    def col(c, w):
        return pl.BlockSpec((None, ts, w), lambda s, b, c=c: (b, s, c))

    def rows_out(w):
        return pl.BlockSpec((None, ts, w), lambda s, b: (b, s, 0))

    def tile_out(r):
        return pl.BlockSpec((None, None, r, ts), lambda s, b: (b, s, 0, 0))

    tab_spec = pl.BlockSpec((2, ts, LANES), lambda s, b: (0, s, 0))
    rot_spec = pl.BlockSpec((LANES, LANES), lambda s, b: (0, 0))
    sd = jax.ShapeDtypeStruct
    return pl.pallas_call(
        _prep_a_kernel,
        grid=(n_t, bsz),
        in_specs=[col(P16_QA, W512), col(P16_KA, W512), col(P16_VA, W512), col(P32_IQ, W512),
                  col(P32_IK, LANES), col(P32_IW, LANES), rot_spec, rot_spec, tab_spec, tab_spec],
        out_specs=[tile_out(A_WIDTH), rows_out(A_WIDTH), tile_out(A_WIDTH), tile_out(IDX_HEADS * IDX_K),
                   rows_out(IDX_K), tile_out(IDX_HEADS)],
        out_shape=[sd((bsz, n_t, A_WIDTH, ts), BF16), sd((bsz, s_len, A_WIDTH), BF16),
                   sd((bsz, n_t, A_WIDTH, ts), BF16), sd((bsz, n_t, IDX_HEADS * IDX_K, ts), BF16),
                   sd((bsz, s_len, IDX_K), BF16), sd((bsz, n_t, IDX_HEADS, ts), F32)],
        compiler_params=_cparams(("parallel", "parallel")),
        name="prep_a",
    )(p16, p16, p16, p32, p32, p32, rope_a[0], rope_i[0], rope_a[1], rope_i[1])


def _rope_tables(s_len, period, reps):
    rot_dims = period // ROPE_FRACTION
    half = rot_dims // 2
    inv = jnp.power(ROPE_THETA, -jnp.arange(half, dtype=F32) * 2.0 / rot_dims)
    ang = jnp.arange(s_len).astype(F32)[:, None] * inv[None, :]
    cos, sin = jnp.cos(ang), jnp.sin(ang)
    pad = jnp.zeros((s_len, period - rot_dims), F32)
    c = jnp.concatenate([cos, cos, pad + 1.0], axis=1)
    s = jnp.concatenate([sin, sin, pad], axis=1)
    lane = np.arange(LANES)
    pos = lane % period
    src, dst = lane[:, None], lane[None, :]
    rot = (np.where((src == dst - half) & (pos[None, :] >= half) & (pos[None, :] < rot_dims), 1.0, 0.0)
           - np.where((src == dst + half) & (pos[None, :] < half), 1.0, 0.0))
    return jnp.asarray(rot, BF16), jnp.stack([jnp.tile(t, (1, reps)) for t in (c, s)])


INT_MAX = np.int32(2 ** 31 - 1)
MASKED = -1e30
SUBLANES = 8


def _dsa_kernel(topk, qt_ref, iqt_ref, iwt_ref, k_ref, vt_ref, ik_ref, o_ref,
                key_ref, acc_ref, jb_ref, v_ref, nge_ref, ml_ref):
    t = DSA_T
    qi = pl.program_id(1)
    n_kt = qi + 1
    s_len = k_ref.shape[0]
    qpos = lax.broadcasted_iota(jnp.int32, (1, t), 1)
    limit = qi * t + (qpos // CHUNK + 1) * CHUNK
    krow = lax.broadcasted_iota(jnp.int32, (t, t), 0)

    def two_per_trip(fn):
        def body(u, carry):
            fn(2 * u)
            fn(2 * u + 1)
            return carry

        lax.fori_loop(0, n_kt // 2, body, 0)

        @pl.when(n_kt % 2 == 1)
        def _():
            fn(n_kt - 1)

    def score_tile(kt):
        start = pl.multiple_of(kt * t, t)
        ikt = ik_ref[pl.ds(start, t), :]
        logits = [jnp.dot(ikt, iqt_ref[h * IDX_K:(h + 1) * IDX_K, :], preferred_element_type=F32)
                  for h in range(IDX_HEADS)]
        score = jnp.zeros((t, t), F32)
        for h in range(IDX_HEADS):
            score = score + jnp.maximum(logits[h], 0.0) * iwt_ref[h:h + 1, :]
        bits = pltpu.bitcast(score, jnp.int32)
        key = bits ^ ((bits >> 31) & jnp.int32(0x7FFFFFFF))
        key_ref[kt] = jnp.where(krow + start < limit, key, INT_MIN)

    two_per_trip(score_tile)

    def count(pred, n=None):
        def body(kt, acc):
            ones = jnp.where(pred(kt), 1.0, 0.0)
            return acc + jnp.sum(ones.reshape(t // SUBLANES, SUBLANES, t), axis=0)
        acc = jnp.zeros((SUBLANES, t), F32)
        if n is None:
            acc = lax.fori_loop(0, n_kt, body, acc)
        else:
            for kt in range(n):
                acc = body(kt, acc)
        return jnp.sum(acc, axis=0, keepdims=True)

    for n in range(1, key_ref.shape[0] + 1):
        @pl.when(n_kt == n)
        def _(n=n):
            def vbody(it, carry):
                lo, n_lo = carry
                cand = lo + lax.shift_left(jnp.int32(1), 31 - it)
                cnt = count(lambda kt: key_ref[kt] >= cand, n)
                ok = cnt >= topk
                return jnp.where(ok, cand, lo), jnp.where(ok, cnt, n_lo)

            lo, n_lo = lax.fori_loop(0, 32, vbody, (jnp.full((1, t), INT_MIN, jnp.int32),
                                                    jnp.full((1, t), float(n * t), F32)))
            v_ref[...] = jnp.broadcast_to(lo, v_ref.shape)
            nge_ref[...] = jnp.broadcast_to(n_lo, nge_ref.shape)

    v = v_ref[0:1, :]
    n_ge = nge_ref[0:1, :]

    jb_ref[...] = jnp.full(jb_ref.shape, INT_MAX, jnp.int32)

    @pl.when(jnp.max(n_ge) > topk)
    def _():
        need = topk - count(lambda kt: key_ref[kt] > v)
        nbits = (s_len - 1).bit_length()

        def jbody(it, jb):
            cand = jb + lax.shift_left(jnp.int32(1), nbits - 1 - it)
            few = count(lambda kt: (key_ref[kt] == v) & (krow + kt * t < cand)) < need
            return jnp.where(few, cand, jb)

        jb = lax.fori_loop(0, nbits, jbody, jnp.zeros((1, t), jnp.int32))
        jb_ref[...] = jnp.broadcast_to(jb, jb_ref.shape)

    jb = jnp.where(v == INT_MIN, -1, jb_ref[0:1, :])

    acc_ref[...] = jnp.zeros_like(acc_ref)

    def att_tile(kt):
        ms = [ml_ref[h:h + 1, :] for h in range(N_HEADS_A)]
        ls = [ml_ref[N_HEADS_A + h:N_HEADS_A + h + 1, :] for h in range(N_HEADS_A)]
        start = pl.multiple_of(kt * t, t)
        key = key_ref[kt]
        tie_bias = jnp.where(key == v, jnp.where(krow + start <= jb, 0.0, MASKED), MASKED)
        bias = jnp.where(key > v, 0.0, tie_bias)
        heads = [slice(h * HEAD_DIM, (h + 1) * HEAD_DIM) for h in range(N_HEADS_A)]
        ss = [jnp.dot(k_ref[pl.ds(start, t), hs], qt_ref[hs, :], preferred_element_type=F32) + bias
              for hs in heads]
        new_ms = [jnp.maximum(ms[h], jnp.max(ss[h], axis=0, keepdims=True)) for h in range(N_HEADS_A)]
        ps = [jnp.exp(ss[h] - new_ms[h]) for h in range(N_HEADS_A)]
        alphas = [jnp.exp(ms[h] - new_ms[h]) for h in range(N_HEADS_A)]
        new_ls = [alphas[h] * ls[h] + jnp.sum(ps[h], axis=0, keepdims=True) for h in range(N_HEADS_A)]
        pvs = [jnp.dot(vt_ref[kt, hs, :], ps[h].astype(BF16), preferred_element_type=F32)
               for h, hs in enumerate(heads)]
        for h in range(N_HEADS_A):
            acc_ref[h] = alphas[h] * acc_ref[h] + pvs[h]
            ml_ref[h:h + 1, :] = new_ms[h]
            ml_ref[N_HEADS_A + h:N_HEADS_A + h + 1, :] = new_ls[h]

    ml_ref[0:N_HEADS_A, :] = jnp.full((N_HEADS_A, t), MASKED, F32)
    ml_ref[N_HEADS_A:2 * N_HEADS_A, :] = jnp.zeros((N_HEADS_A, t), F32)
    two_per_trip(att_tile)
    for h in range(N_HEADS_A):
        l = ml_ref[N_HEADS_A + h:N_HEADS_A + h + 1, :]
        o_ref[:, h * HEAD_DIM:(h + 1) * HEAD_DIM] = (acc_ref[h] / l).T.astype(o_ref.dtype)


def _dsa(qt, ka, vt, iqt, ik_ext, iwt):
    bsz, s_len, _ = ka.shape
    t = DSA_T
    n_t = s_len // t
    topk = min(IDX_TOPK_MAX, s_len // 4)

    def qtile(r):
        return pl.BlockSpec((None, None, r, t), lambda b, i: (b, i, 0, 0))

    return pl.pallas_call(
        functools.partial(_dsa_kernel, topk),
        grid=(bsz, n_t),
        in_specs=[qtile(A_WIDTH), qtile(IDX_HEADS * IDX_K), qtile(IDX_HEADS),
                  pl.BlockSpec((None, s_len, A_WIDTH), lambda b, i: (b, 0, 0)),
                  pl.BlockSpec((None, n_t, A_WIDTH, t), lambda b, i: (b, 0, 0, 0)),
                  pl.BlockSpec((None, s_len, IDX_K), lambda b, i: (b, 0, 0))],
        out_specs=pl.BlockSpec((None, t, A_WIDTH), lambda b, i: (b, i, 0)),
        out_shape=jax.ShapeDtypeStruct((bsz, s_len, A_WIDTH), BF16),
        scratch_shapes=[pltpu.VMEM((n_t, t, t), jnp.int32),
                        pltpu.VMEM((N_HEADS_A, HEAD_DIM, t), F32),
                        pltpu.VMEM((SUBLANES, t), jnp.int32),
                        pltpu.VMEM((SUBLANES, t), jnp.int32),
                        pltpu.VMEM((SUBLANES, t), F32),
                        pltpu.VMEM((2 * N_HEADS_A, t), F32)],
        compiler_params=_cparams(("parallel", "arbitrary")),
        name="dsa",
    )(qt, iqt, iwt, ka, vt, ik_ext)


GLA_ROWS = 256


GLA_BLK = SUBLANES
GLA_NBLK = CHUNK // GLA_BLK
GLA_PAIRS = [(i, j) for i in range(GLA_NBLK) for j in range(i)]


def _split3(x):
    hi = x.astype(BF16)
    r1 = x - hi.astype(F32)
    mid = r1.astype(BF16)
    return hi, mid, (r1 - mid.astype(F32)).astype(BF16)


def _gla_cumsum_matrix():
    t = np.arange(CHUNK)
    blk = t // GLA_BLK
    incl = t[None, :] <= t[:, None]
    start = t[None, :] < (GLA_BLK * blk)[:, None]
    end = t[None, :] < (GLA_BLK * (blk + 1))[:, None]
    bounds = t[None, :] < (GLA_BLK * np.arange(GLA_NBLK))[:, None]
    pad = np.zeros((GLA_BLK, CHUNK), bool)
    m = np.concatenate([incl, start, end, bounds, pad], axis=0).astype(np.float32)
    return jnp.asarray(np.concatenate([m, m, m], axis=1), BF16)


def _gla_kernel(q_ref, k_ref, v_ref, og_ref, gl_ref, wg_ref, bg_ref, gn_ref, cm_ref, sg_ref, o_ref,
                st_ref, lg_ref, b_ref, kc_ref):
    @pl.when(pl.program_id(1) == 0)
    def _():
        st_ref[...] = jnp.zeros_like(st_ref)

    g_hi, g_mid, g_lo = (p.astype(F32) for p in _split3(gl_ref[...]))
    r = GLA_GATE_RANK
    packed = (g_hi + pltpu.roll(g_hi, r, 1) + pltpu.roll(g_mid, 2 * r, 1) + pltpu.roll(g_hi, 3 * r, 1)
              + pltpu.roll(g_lo, 4 * r, 1) + pltpu.roll(g_mid, 5 * r, 1))
    gate = jnp.dot(packed.astype(BF16), wg_ref[...], preferred_element_type=F32) + bg_ref[...]
    lg_ref[...] = _log_sigmoid(gate) * (1.0 / GLA_TAU)

    sub = lax.broadcasted_iota(jnp.int32, (GLA_BLK, LANES), 0)
    lane_c = lax.broadcasted_iota(jnp.int32, (GLA_BLK, CHUNK), 1)
    pairs = [slice(p * LANES, (p + 1) * LANES) for p in range(N_HEADS_B * GLA_DK // LANES)]
    first = lax.broadcasted_iota(jnp.int32, (CHUNK, LANES), 1) < GLA_DK

    def chunk(ci, carry):
        rows = pl.ds(ci * CHUNK, CHUNK)
        cums = []
        for ps in pairs:
            parts = _split3(lg_ref[rows, ps])
            cums.append(jnp.dot(cm_ref[...], jnp.concatenate(parts, axis=0), preferred_element_type=F32))
        for p, ps in enumerate(pairs):
            b_ref[rows, ps] = cums[p][0:CHUNK]
        kc_ref[rows, :] = k_ref[rows, :].astype(F32)
        lhss, kbs, xalls, qes, kds, bls = [], [], [], [], [], []
        for p, ps in enumerate(pairs):
            b = cums[p][0:CHUNK]
            b_start = cums[p][CHUNK:2 * CHUNK]
            b_end = cums[p][2 * CHUNK:3 * CHUNK]
            bound = cums[p][3 * CHUNK:3 * CHUNK + GLA_NBLK]
            q = q_ref[rows, ps].astype(F32) * GLA_DK ** -0.5
            k = kc_ref[rows, ps]
            qb = q * jnp.exp(b - b_start)
            kb = k * jnp.exp(b_end - b)
            kbs.append((jnp.where(first, kb, 0.0).astype(BF16), jnp.where(first, 0.0, kb).astype(BF16)))
            dec = [jnp.exp(jnp.where(sub > j, bound - bound[j + 1:j + 2], -jnp.inf))
                   for j in range(GLA_NBLK - 1)]
            lhss.append(jnp.concatenate([qb[i * GLA_BLK:(i + 1) * GLA_BLK] * dec[j][i:i + 1]
                                         for i, j in GLA_PAIRS], axis=0).astype(BF16))
            xs = []
            for s in range(CHUNK):
                blk = slice(s // GLA_BLK * GLA_BLK, (s // GLA_BLK + 1) * GLA_BLK)
                srow = pl.ds(ci * CHUNK + s, 1)
                ks = jnp.broadcast_to(kc_ref[srow, ps], (GLA_BLK, LANES))
                bs = jnp.broadcast_to(b_ref[srow, ps], (GLA_BLK, LANES))
                xs.append(q[blk] * ks * jnp.exp(jnp.where(sub >= s % GLA_BLK, b[blk] - bs, -jnp.inf)))
            xalls.append(jnp.concatenate(xs, axis=0).astype(BF16))
            bl = b[CHUNK - 1:CHUNK]
            kd = k * jnp.exp(bl - b)
            qes += [(q * jnp.exp(b)).astype(BF16)] * 2
            kds += [jnp.where(first, kd, 0.0).astype(BF16), jnp.where(first, 0.0, kd).astype(BF16)]
            bls += [bl, bl]
        cross = [[_dot_t(lhss[p], kbs[p][e]) for e in range(2)] for p in range(len(pairs))]
        sums = [jnp.dot(xalls[p], sg_ref[...], preferred_element_type=F32) for p in range(len(pairs))]
        atts = []
        for p in range(len(pairs)):
            for e in range(2):
                att_rows = []
                for i in range(GLA_NBLK):
                    att = jnp.zeros((GLA_BLK, CHUNK), F32)
                    for g, (gi, j) in enumerate(GLA_PAIRS):
                        if gi == i:
                            att = jnp.where(lane_c // GLA_BLK == j,
                                            cross[p][e][g * GLA_BLK:(g + 1) * GLA_BLK], att)
                    for s in range(i * GLA_BLK, (i + 1) * GLA_BLK):
                        col = sums[p][s * GLA_BLK:(s + 1) * GLA_BLK, e * LANES:e * LANES + CHUNK]
                        att = jnp.where(lane_c == s, col, att)
                    att_rows.append(att)
                atts.append(jnp.concatenate(att_rows, axis=0).astype(BF16))
        for h in range(N_HEADS_B):
            hs = slice(h * GLA_DV, (h + 1) * GLA_DV)
            v = v_ref[rows, hs]
            st = st_ref[h]
            o = _dot_t(qes[h], st.astype(BF16)) + jnp.dot(atts[h], v, preferred_element_type=F32)
            st_ref[h] = st * jnp.exp(bls[h]) + lax.dot_general(
                v, kds[h], (((0,), (0,)), ((), ())), preferred_element_type=F32)
            og = og_ref[rows, hs].astype(F32)
            o_ref[rows, hs] = (_rms(o, gn_ref[...]) * (og * jax.nn.sigmoid(og))).astype(o_ref.dtype)
        return carry

    for ci in range(q_ref.shape[0] // CHUNK):
        chunk(ci, 0)


def _gla(p16, p32, wg, bg, gn):
    bsz, s_len, _ = p16.shape

    def col(c, w):
        return pl.BlockSpec((None, GLA_ROWS, w), lambda b, s, c=c: (b, s, c))

    cm = _gla_cumsum_matrix()
    qk_w = N_HEADS_B * GLA_DK
    lane = np.arange(LANES)[:, None] // GLA_DK
    sg = jnp.asarray(lane == np.arange(2 * LANES)[None, :] // LANES, BF16)
    return pl.pallas_call(
        _gla_kernel,
        grid=(bsz, s_len // GLA_ROWS),
        in_specs=[col(2 * P16_QKB, qk_w), col(2 * P16_QKB + 1, qk_w), col(P16_VB, W512), col(P16_OG, W512),
                  col(P32_GLOW, LANES),
                  pl.BlockSpec((LANES, qk_w), lambda b, s: (0, 0)),
                  pl.BlockSpec((1, qk_w), lambda b, s: (0, 0)),
                  pl.BlockSpec((1, GLA_DV), lambda b, s: (0, 0)),
                  pl.BlockSpec(cm.shape, lambda b, s: (0, 0)),
                  pl.BlockSpec(sg.shape, lambda b, s: (0, 0))],
        out_specs=pl.BlockSpec((None, GLA_ROWS, B_WIDTH), lambda b, s: (b, s, 0)),
        out_shape=jax.ShapeDtypeStruct((bsz, s_len, B_WIDTH), BF16),
        scratch_shapes=[pltpu.VMEM((N_HEADS_B, GLA_DV, LANES), F32),
                        pltpu.VMEM((GLA_ROWS, qk_w), F32),
                        pltpu.VMEM((GLA_ROWS, qk_w), F32),
                        pltpu.VMEM((GLA_ROWS, qk_w), F32)],
        compiler_params=_cparams(("parallel", "arbitrary")),
        name="gla",
    )(p16, p16, p16, p16, p32, wg, bg, gn.reshape(1, GLA_DV), cm, sg)


def _gla_gate_weights(w_gate_up):
    hi, mid, lo = _split3(w_gate_up)
    w = jnp.concatenate([hi, mid, hi, lo, hi, mid], axis=0)
    return jnp.pad(w, ((0, LANES - w.shape[0]), (0, 0)))


SB_T = 256
SB_HEADS = 2
SB_W = SB_HEADS * HEAD_DIM
LOG2E = 1.4426950408889634


def _sb_kernel(q_ref, kn_ref, v_ref, cw_ref, o_ref, acc_ref, car_ref, zn_ref):
    def tile(i, carry):
        _sb_tile(i, q_ref, kn_ref, v_ref, cw_ref, o_ref, acc_ref, car_ref, zn_ref)
        return carry

    lax.fori_loop(0, q_ref.shape[0] // SB_T, tile, 0)


def _sb_tile(i, q_ref, kn_ref, v_ref, cw_ref, o_ref, acc_ref, car_ref, zn_ref):
    heads = [slice(h * HEAD_DIM, (h + 1) * HEAD_DIM) for h in range(SB_HEADS)]
    q_rows = pl.ds(pl.multiple_of(i * SB_T, SB_T), SB_T)
    qs = [q_ref[q_rows, hs] for hs in heads]

    def logits(j):
        start = pl.multiple_of(j * SB_T, SB_T)
        return [_dot_t(qs[h], kn_ref[pl.ds(start, SB_T), hs]) for h, hs in enumerate(heads)]

    def block(j, zs, next_slot, diag):
        start = pl.multiple_of(j * SB_T, SB_T)
        if diag:
            strict = (lax.broadcasted_iota(jnp.int32, (SB_T, SB_T), 1)
                      < lax.broadcasted_iota(jnp.int32, (SB_T, SB_T), 0))
        for h, z in enumerate(logits(jnp.maximum(j - 1, 0))):
            zn_ref[next_slot * SB_HEADS + h] = z
        hls = []
        for z in zs:
            sp = jnp.maximum(z, 0.0) + jnp.log2(1.0 + jnp.exp2(-jnp.abs(z)))
            if diag:
                sp = jnp.where(strict, sp, 0.0)
            hls.append(sp.astype(BF16))
        rs = [jnp.dot(hl, cw_ref[...], preferred_element_type=F32) for hl in hls]
        avs = []
        for h in range(SB_HEADS):
            car = car_ref[h]
            a = jnp.exp2(zs[h] - rs[h] - jnp.concatenate([car] * (SB_T // LANES), axis=1))
            if diag:
                a = jnp.where(strict, a, 0.0)
            avs.append(a.astype(BF16))
            car_ref[h] = car + jnp.broadcast_to(rs[h][:, 0:1], car.shape)
        for h, hs in enumerate(heads):
            acc_ref[:, hs] += jnp.dot(avs[h], v_ref[pl.ds(start, SB_T), hs], preferred_element_type=F32)

    acc_ref[...] = jnp.zeros_like(acc_ref)
    car_ref[...] = jnp.zeros_like(car_ref)

    def trip(j, slot):
        zs = [zn_ref[slot * SB_HEADS + h] for h in range(SB_HEADS)]
        block(j, zs, 1 - slot, False)

    odd = i % 2

    @pl.when(odd == 0)
    def _():
        block(i, logits(i), 0, True)

    @pl.when(odd == 1)
    def _():
        block(i, logits(i), 1, True)
        trip(i - 1, 1)

    def body(u, carry):
        j = i - 1 - odd - 2 * u
        trip(j, 0)
        trip(j - 1, 1)
        return carry

    lax.fori_loop(0, i // 2, body, 0)
    o_ref[q_rows, :] = acc_ref[...].astype(o_ref.dtype)


def _stick_breaking(qkv):
    bsz, s_len, _ = qkv.shape
    cum_w = (jnp.arange(SB_T)[:, None] >= jnp.arange(SB_T)[None, :]).astype(BF16)
    n_grp = N_HEADS_C // SB_HEADS
    return pl.pallas_call(
        _sb_kernel,
        grid=(bsz, n_grp),
        in_specs=[pl.BlockSpec((None, s_len, SB_W), lambda b, h: (b, 0, h)),
                  pl.BlockSpec((None, s_len, SB_W), lambda b, h: (b, 0, n_grp + h)),
                  pl.BlockSpec((None, s_len, SB_W), lambda b, h: (b, 0, 2 * n_grp + h)),
                  pl.BlockSpec((SB_T, SB_T), lambda b, h: (0, 0))],
        out_specs=pl.BlockSpec((None, s_len, SB_W), lambda b, h: (b, 0, h)),
        out_shape=jax.ShapeDtypeStruct((bsz, s_len, C_WIDTH), BF16),
        scratch_shapes=[pltpu.VMEM((SB_T, SB_W), F32), pltpu.VMEM((SB_HEADS, SB_T, LANES), F32),
                        pltpu.VMEM((2 * SB_HEADS, SB_T, SB_T), F32)],
        compiler_params=_cparams(("parallel", "parallel")),
        name="stick_breaking",
    )(qkv, qkv, qkv, cum_w)


def _pad_heads(w, n_heads, width):
    lead = w.shape[:-1]
    d = w.shape[-1] // n_heads
    w = w.reshape(*lead, n_heads, d)
    w = jnp.pad(w, [(0, 0)] * len(lead) + [(0, 0), (0, width - d)])
    return w.reshape(*lead, n_heads * width)


def _pad_cols(w, width):
    return jnp.pad(w, [(0, 0)] * (w.ndim - 1) + [(0, width - w.shape[-1])])


def _layer0_weights(w_in):
    parts, o = [], 0
    for s in AB_SIZES:
        parts.append(w_in[:, o:o + s])
        o += s
    qa, ka, va, iq, ik, iw, qb, kb, vb, glow, og = parts
    cols = [qa, ka, va, qb, kb, vb, og,
            iq, _pad_cols(ik, LANES), _pad_cols(iw, LANES), _pad_cols(glow, 2 * LANES)]
    return jnp.concatenate(cols, axis=1).astype(BF16)


def kernel(x, g_mix, g_ffn, w_in_ab, gq_a, gk_a, w_gate_up, b_gate, g_gla, w_out_ab,
           w_in_c, gq_c, gk_c, w_out_c, w_up, w_down):
    bsz, s_len, d = x.shape
    m = bsz * s_len
    h = x.reshape(m, d)

    qk_gains = jnp.stack([gq_a[0] * HEAD_DIM ** -0.5, gk_a[0]])
    p16, p32 = _norm_matmul(h, g_mix[0], _layer0_weights(w_in_ab[0]), qk_gains, tm=PROJ_TM, tn=W512, n16=N_P16)
    p16 = p16.reshape(bsz, s_len, N_P16 * W512)
    p32 = p32.reshape(bsz, s_len, N_P32 * W512)
    rope_a = _rope_tables(s_len, HEAD_DIM, 1)
    rope_i = _rope_tables(s_len, IDX_DIM, LANES // IDX_DIM)
    oa = _dsa(*_prep_a(p16, p32, rope_a, rope_i))
    wg = _gla_gate_weights(w_gate_up[0])
    bg = b_gate[0].reshape(1, N_HEADS_B * GLA_DK)
    ob = _gla(p16, p32, wg, bg, g_gla[0])
    w_o = w_out_ab[0].astype(BF16)
    h = _proj_mlp(h, [oa.reshape(m, A_WIDTH), ob.reshape(m, B_WIDTH)], [w_o[:A_WIDTH], w_o[A_WIDTH:]],
                  g_ffn[0], w_up[0].astype(BF16), w_down[0].astype(BF16), tm=MLP_TM, tf=MLP_TF)

    qk_gains = jnp.stack([gq_c[0] * (HEAD_DIM ** -0.5 * LOG2E), gk_c[0]])
    qkv = _norm_matmul(h, g_mix[1], w_in_c[0].astype(BF16), qk_gains, tm=PROJ_TM, tn=C_WIDTH)
    oc = _stick_breaking(qkv.reshape(bsz, s_len, 3 * C_WIDTH))
    h = _proj_mlp(h, [oc.reshape(m, C_WIDTH)], [w_out_c[0].astype(BF16)],
                  g_ffn[1], w_up[1].astype(BF16), w_down[1].astype(BF16), tm=MLP_TM, tf=MLP_TF)
    return h.reshape(bsz, s_len, d)
```

```python
import functools

import numpy as np
import jax
import jax.numpy as jnp
from jax import lax
from jax.experimental import pallas as pl
from jax.experimental.pallas import tpu as pltpu

F32 = jnp.float32
BF16 = jnp.bfloat16

D_MODEL = 1024
CHUNK = 64
HEAD_DIM = 128
EPS = 1e-6
N_HEADS_A = 4
A_WIDTH = N_HEADS_A * HEAD_DIM
IDX_HEADS = 8
IDX_DIM = 64
IDX_TOPK_MAX = 256
N_HEADS_B = 4
GLA_DK = 64
GLA_DV = 128
GLA_GATE_RANK = 16
GLA_TAU = 16.0
B_WIDTH = N_HEADS_B * GLA_DV
N_HEADS_C = 8
C_WIDTH = N_HEADS_C * HEAD_DIM
ROPE_THETA = 500000.0
ROPE_FRACTION = 4
D_FF = 4 * D_MODEL
AB_SIZES = (A_WIDTH, A_WIDTH, A_WIDTH, IDX_HEADS * IDX_DIM, IDX_DIM, IDX_HEADS,
            N_HEADS_B * GLA_DK, N_HEADS_B * GLA_DK, B_WIDTH, GLA_GATE_RANK, B_WIDTH)

LANES = 128
VMEM_LIMIT = 48 * 1024 * 1024
MLP_TM, MLP_TF = 512, 512
PROJ_TM = 512

W512 = 512
P_QA, P_KA, P_VA, P_QKB, P_VB, P_OG, P_IQ, P_MISC = range(8)

INT_MIN = np.int32(-2 ** 31)


def _cparams(sem):
    return pltpu.CompilerParams(dimension_semantics=sem, vmem_limit_bytes=VMEM_LIMIT)


def _rms(x, g):
    return x * lax.rsqrt(jnp.mean(x * x, axis=-1, keepdims=True) + EPS) * g


def _log_sigmoid(x):
    return jnp.minimum(x, 0.0) - jnp.log1p(jnp.exp(-jnp.abs(x)))


def _dot_t(a, b):
    return lax.dot_general(a, b, (((1,), (1,)), ((), ())), preferred_element_type=F32)


def _norm_matmul_kernel(n16, tn, n_qk, x_ref, g_ref, w_ref, hg_ref, *o_refs):
    xn = _rms(x_ref[...], g_ref[...]).astype(BF16)
    for c in range(w_ref.shape[1] // tn):
        y = jnp.dot(xn, w_ref[:, c * tn:(c + 1) * tn], preferred_element_type=F32)
        if c < n_qk:
            y = _head_rms(y, hg_ref[c])
        if c < n16:
            o_refs[0][:, c * tn:(c + 1) * tn] = y.astype(BF16)
        else:
            o_refs[1][:, (c - n16) * tn:(c - n16 + 1) * tn] = y


def _norm_matmul(x, g, w, head_gains, tm, tn, n16=None):
    m, d = x.shape
    n_qk = head_gains.shape[0]
    n = w.shape[1]
    split = n16 is not None
    n16 = n16 if split else n // tn
    out_specs = [pl.BlockSpec((tm, n16 * tn), lambda i: (i, 0))]
    out_shape = [jax.ShapeDtypeStruct((m, n16 * tn), BF16)]
    if split:
        out_specs.append(pl.BlockSpec((tm, n - n16 * tn), lambda i: (i, 0)))
        out_shape.append(jax.ShapeDtypeStruct((m, n - n16 * tn), F32))
    out = pl.pallas_call(
        functools.partial(_norm_matmul_kernel, n16, tn, n_qk),
        grid=(m // tm,),
        in_specs=[pl.BlockSpec((tm, d), lambda i: (i, 0)),
                  pl.BlockSpec((1, d), lambda i: (0, 0)),
                  pl.BlockSpec((d, n), lambda i: (0, 0)),
                  pl.BlockSpec((n_qk, 1, HEAD_DIM), lambda i: (0, 0, 0))],
        out_specs=out_specs,
        out_shape=out_shape,
        compiler_params=_cparams(("parallel",)),
        name="norm_matmul",
    )(x, g.reshape(1, d), w, head_gains.reshape(n_qk, 1, HEAD_DIM))
    return tuple(out) if split else out[0]


def _proj_mlp_kernel(n_a, tf, h_ref, *refs):
    a_refs = refs[:n_a]
    wo_refs = refs[n_a:2 * n_a]
    g_ref, wu_ref, wd_ref, o_ref, xn_ref = refs[2 * n_a:]
    h1 = h_ref[...]
    for a_ref, wo_ref in zip(a_refs, wo_refs):
        h1 = h1 + jnp.dot(a_ref[...], wo_ref[...], preferred_element_type=F32)
    o_ref[...] = h1
    xn_ref[...] = _rms(h1, g_ref[...]).astype(BF16)
    for c in range(wu_ref.shape[1] // tf):
        u = jnp.dot(xn_ref[...], wu_ref[:, c * tf:(c + 1) * tf], preferred_element_type=F32)
        u = jnp.square(jnp.maximum(u, 0.0)).astype(BF16)
        o_ref[...] += jnp.dot(u, wd_ref[c * tf:(c + 1) * tf, :], preferred_element_type=F32)


def _proj_mlp(h, a_list, wo_list, g, w_up, w_down, tm, tf):
    m, d = h.shape
    n_a = len(a_list)

    def resident(w):
        return pl.BlockSpec(w.shape, lambda i: (0, 0))

    in_specs = [pl.BlockSpec((tm, d), lambda i: (i, 0))]
    in_specs += [pl.BlockSpec((tm, a.shape[1]), lambda i: (i, 0)) for a in a_list]
    in_specs += [resident(w) for w in wo_list]
    in_specs += [pl.BlockSpec((1, d), lambda i: (0, 0)), resident(w_up), resident(w_down)]
    return pl.pallas_call(
        functools.partial(_proj_mlp_kernel, n_a, tf),
        grid=(m // tm,),
        in_specs=in_specs,
        out_specs=pl.BlockSpec((tm, d), lambda i: (i, 0)),
        out_shape=jax.ShapeDtypeStruct((m, d), F32),
        scratch_shapes=[pltpu.VMEM((tm, d), BF16)],
        compiler_params=_cparams(("parallel",)),
        name="proj_mlp",
    )(h, *a_list, *wo_list, g.reshape(1, d), w_up, w_down)


def _rope(x, rot_ref, tab_ref):
    out = []
    for s in range(x.shape[1] // LANES):
        xs = x[:, s * LANES:(s + 1) * LANES]
        if xs.dtype == BF16:
            r = jnp.dot(xs, rot_ref[...], preferred_element_type=F32)
        else:
            hi, lo = _hi_lo(xs)
            r = (jnp.dot(hi, rot_ref[...], preferred_element_type=F32)
                 + jnp.dot(lo, rot_ref[...], preferred_element_type=F32))
        out.append(xs.astype(F32) * tab_ref[0] + r * tab_ref[1])
    return out


def _head_rms(x, g):
    out = []
    for s in range(x.shape[1] // HEAD_DIM):
        out.append(_rms(x[:, s * HEAD_DIM:(s + 1) * HEAD_DIM], g))
    return jnp.concatenate(out, axis=1)


DSA_T = 256
IDX_K = 4 * IDX_DIM


def _hi_lo(x):
    hi = x.astype(BF16)
    return hi, (x - hi.astype(F32)).astype(BF16)


def _proj0_kernel(x_ref, g_ref, w_ref, hg_ref, ra_ref, ri_ref, ta_ref, ti_ref,
                  qt_ref, ko_ref, vt_ref, iqt_ref, iko_ref, iwt_ref, gla_ref, gl_ref):
    n_sub = x_ref.shape[0] // DSA_T
    xn = _rms(x_ref[...], g_ref[...]).astype(BF16)

    def col(c):
        return jnp.dot(xn, w_ref[:, c * W512:(c + 1) * W512], preferred_element_type=F32)

    def sub_t(a, s):
        return a[s * DSA_T:(s + 1) * DSA_T].T

    q = jnp.concatenate(_rope(_head_rms(col(P_QA), hg_ref[0]).astype(BF16), ra_ref, ta_ref), axis=1)
    for s in range(n_sub):
        qt_ref[s] = sub_t(q, s).astype(BF16)
    k = _head_rms(col(P_KA), hg_ref[1]).astype(BF16)
    ko_ref[...] = jnp.concatenate(_rope(k, ra_ref, ta_ref), axis=1).astype(BF16)
    v = col(P_VA)
    for s in range(n_sub):
        vt_ref[s] = sub_t(v, s).astype(BF16)
    for c in (P_QKB, P_VB, P_OG):
        gla_ref[:, (c - P_QKB) * W512:(c - P_QKB + 1) * W512] = col(c).astype(BF16)
    iq = jnp.concatenate(_rope(col(P_IQ), ri_ref, ti_ref), axis=1) * IDX_DIM ** -0.5
    for s in range(n_sub):
        hi, lo = _hi_lo(sub_t(iq, s))
        for h in range(IDX_HEADS):
            rows = slice(h * IDX_DIM, (h + 1) * IDX_DIM)
            for part, val in enumerate((hi, hi, lo, lo)):
                base = h * IDX_K + part * IDX_DIM
                iqt_ref[s, base:base + IDX_DIM, :] = val[rows, :]
    misc = col(P_MISC)
    ik = _rope(misc[:, 0:LANES], ri_ref, ti_ref)[0]
    ik_hi, ik_lo = _hi_lo(ik)
    ika = ik_hi.astype(F32) + pltpu.roll(ik_lo.astype(F32), IDX_DIM, 1)
    iko_ref[...] = jnp.concatenate([ika, ika], axis=1).astype(BF16)
    iw = misc[:, LANES:2 * LANES] * IDX_HEADS ** -0.5
    for s in range(n_sub):
        iwt_ref[s] = sub_t(iw, s)[0:IDX_HEADS, :]
    gl_ref[...] = misc[:, 2 * LANES:3 * LANES]


def _proj0(x, g, w, head_gains, rope_a, rope_i, bsz, s_len, tm):
    m, d = x.shape
    n_t = s_len // DSA_T
    spb = s_len // tm
    n_sub = tm // DSA_T

    def rows_out(w):
        return pl.BlockSpec((None, tm, w), lambda i: (i // spb, i % spb, 0))

    def tile_out(r):
        return pl.BlockSpec((None, n_sub, r, DSA_T), lambda i: (i // spb, i % spb, 0, 0))

    tab_spec = pl.BlockSpec((2, tm, LANES), lambda i: (0, i % spb, 0))
    rot_spec = pl.BlockSpec((LANES, LANES), lambda i: (0, 0))
    sd = jax.ShapeDtypeStruct
    return pl.pallas_call(
        _proj0_kernel,
        grid=(m // tm,),
        in_specs=[pl.BlockSpec((tm, d), lambda i: (i, 0)),
                  pl.BlockSpec((1, d), lambda i: (0, 0)),
                  pl.BlockSpec(w.shape, lambda i: (0, 0)),
                  pl.BlockSpec((2, 1, HEAD_DIM), lambda i: (0, 0, 0)),
                  rot_spec, rot_spec, tab_spec, tab_spec],
        out_specs=[tile_out(A_WIDTH), rows_out(A_WIDTH), tile_out(A_WIDTH), tile_out(IDX_HEADS * IDX_K),
                   rows_out(IDX_K), tile_out(IDX_HEADS), rows_out(3 * W512), rows_out(LANES)],
        out_shape=[sd((bsz, n_t, A_WIDTH, DSA_T), BF16), sd((bsz, s_len, A_WIDTH), BF16),
                   sd((bsz, n_t, A_WIDTH, DSA_T), BF16), sd((bsz, n_t, IDX_HEADS * IDX_K, DSA_T), BF16),
                   sd((bsz, s_len, IDX_K), BF16), sd((bsz, n_t, IDX_HEADS, DSA_T), F32),
                   sd((bsz, s_len, 3 * W512), BF16), sd((bsz, s_len, LANES), F32)],
        compiler_params=_cparams(("parallel",)),
        name="proj0",
    )(x, g.reshape(1, d), w, head_gains.reshape(2, 1, HEAD_DIM), rope_a[0], rope_i[0], rope_a[1], rope_i[1])


def _rope_tables(s_len, period, reps):
    rot_dims = period // ROPE_FRACTION
    half = rot_dims // 2
    inv = jnp.power(ROPE_THETA, -jnp.arange(half, dtype=F32) * 2.0 / rot_dims)
    ang = jnp.arange(s_len).astype(F32)[:, None] * inv[None, :]
    cos, sin = jnp.cos(ang), jnp.sin(ang)
    pad = jnp.zeros((s_len, period - rot_dims), F32)
    c = jnp.concatenate([cos, cos, pad + 1.0], axis=1)
    s = jnp.concatenate([sin, sin, pad], axis=1)
    lane = np.arange(LANES)
    pos = lane % period
    src, dst = lane[:, None], lane[None, :]
    rot = (np.where((src == dst - half) & (pos[None, :] >= half) & (pos[None, :] < rot_dims), 1.0, 0.0)
           - np.where((src == dst + half) & (pos[None, :] < half), 1.0, 0.0))
    return jnp.asarray(rot, BF16), jnp.stack([jnp.tile(t, (1, reps)) for t in (c, s)])


INT_MAX = np.int32(2 ** 31 - 1)
MASKED = -1e30
SUBLANES = 8


def _dsa_kernel(topk, qt_ref, iqt_ref, iwt_ref, k_ref, vt_ref, ik_ref, o_ref,
                key_ref, acc_ref, jb_ref, v_ref, nge_ref, ml_ref):
    t = DSA_T
    qi = pl.program_id(1)
    n_kt = qi + 1
    s_len = k_ref.shape[0]
    qpos = lax.broadcasted_iota(jnp.int32, (1, t), 1)
    limit = qi * t + (qpos // CHUNK + 1) * CHUNK
    krow = lax.broadcasted_iota(jnp.int32, (t, t), 0)

    def two_per_trip(fn):
        def body(u, carry):
            fn(2 * u)
            fn(2 * u + 1)
            return carry

        lax.fori_loop(0, n_kt // 2, body, 0)

        @pl.when(n_kt % 2 == 1)
        def _():
            fn(n_kt - 1)

    def score_tile(kt):
        start = pl.multiple_of(kt * t, t)
        ikt = ik_ref[pl.ds(start, t), :]
        logits = [jnp.dot(ikt, iqt_ref[h * IDX_K:(h + 1) * IDX_K, :], preferred_element_type=F32)
                  for h in range(IDX_HEADS)]
        score = jnp.zeros((t, t), F32)
        for h in range(IDX_HEADS):
            score = score + jnp.maximum(logits[h], 0.0) * iwt_ref[h:h + 1, :]
        bits = pltpu.bitcast(score, jnp.int32)
        key = bits ^ ((bits >> 31) & jnp.int32(0x7FFFFFFF))
        key_ref[kt] = jnp.where(krow + start < limit, key, INT_MIN)

    two_per_trip(score_tile)

    def count(pred, n=None):
        def body(kt, acc):
            ones = jnp.where(pred(kt), 1.0, 0.0)
            return acc + jnp.sum(ones.reshape(t // SUBLANES, SUBLANES, t), axis=0)
        acc = jnp.zeros((SUBLANES, t), F32)
        if n is None:
            acc = lax.fori_loop(0, n_kt, body, acc)
        else:
            for kt in range(n):
                acc = body(kt, acc)
        return jnp.sum(acc, axis=0, keepdims=True)

    for n in range(1, key_ref.shape[0] + 1):
        @pl.when(n_kt == n)
        def _(n=n):
            def vbody(it, carry):
                lo, n_lo = carry
                cand = lo + lax.shift_left(jnp.int32(1), 31 - it)
                cnt = count(lambda kt: key_ref[kt] >= cand, n)
                ok = cnt >= topk
                return jnp.where(ok, cand, lo), jnp.where(ok, cnt, n_lo)

            lo, n_lo = lax.fori_loop(0, 32, vbody, (jnp.full((1, t), INT_MIN, jnp.int32),
                                                    jnp.full((1, t), float(n * t), F32)))
            v_ref[...] = jnp.broadcast_to(lo, v_ref.shape)
            nge_ref[...] = jnp.broadcast_to(n_lo, nge_ref.shape)

    v = v_ref[0:1, :]
    n_ge = nge_ref[0:1, :]

    jb_ref[...] = jnp.full(jb_ref.shape, INT_MAX, jnp.int32)

    @pl.when(jnp.max(n_ge) > topk)
    def _():
        need = topk - count(lambda kt: key_ref[kt] > v)
        nbits = (s_len - 1).bit_length()

        def jbody(it, jb):
            cand = jb + lax.shift_left(jnp.int32(1), nbits - 1 - it)
            few = count(lambda kt: (key_ref[kt] == v) & (krow + kt * t < cand)) < need
            return jnp.where(few, cand, jb)

        jb = lax.fori_loop(0, nbits, jbody, jnp.zeros((1, t), jnp.int32))
        jb_ref[...] = jnp.broadcast_to(jb, jb_ref.shape)

    jb = jnp.where(v == INT_MIN, -1, jb_ref[0:1, :])

    acc_ref[...] = jnp.zeros_like(acc_ref)

    def att_tile(kt):
        ms = [ml_ref[h:h + 1, :] for h in range(N_HEADS_A)]
        ls = [ml_ref[N_HEADS_A + h:N_HEADS_A + h + 1, :] for h in range(N_HEADS_A)]
        start = pl.multiple_of(kt * t, t)
        key = key_ref[kt]
        tie_bias = jnp.where(key == v, jnp.where(krow + start <= jb, 0.0, MASKED), MASKED)
        bias = jnp.where(key > v, 0.0, tie_bias)
        heads = [slice(h * HEAD_DIM, (h + 1) * HEAD_DIM) for h in range(N_HEADS_A)]
        ss = [jnp.dot(k_ref[pl.ds(start, t), hs], qt_ref[hs, :], preferred_element_type=F32) + bias
              for hs in heads]
        new_ms = [jnp.maximum(ms[h], jnp.max(ss[h], axis=0, keepdims=True)) for h in range(N_HEADS_A)]
        ps = [jnp.exp(ss[h] - new_ms[h]) for h in range(N_HEADS_A)]
        alphas = [jnp.exp(ms[h] - new_ms[h]) for h in range(N_HEADS_A)]
        new_ls = [alphas[h] * ls[h] + jnp.sum(ps[h], axis=0, keepdims=True) for h in range(N_HEADS_A)]
        pvs = [jnp.dot(vt_ref[kt, hs, :], ps[h].astype(BF16), preferred_element_type=F32)
               for h, hs in enumerate(heads)]
        for h in range(N_HEADS_A):
            acc_ref[h] = alphas[h] * acc_ref[h] + pvs[h]
            ml_ref[h:h + 1, :] = new_ms[h]
            ml_ref[N_HEADS_A + h:N_HEADS_A + h + 1, :] = new_ls[h]

    ml_ref[0:N_HEADS_A, :] = jnp.full((N_HEADS_A, t), MASKED, F32)
    ml_ref[N_HEADS_A:2 * N_HEADS_A, :] = jnp.zeros((N_HEADS_A, t), F32)
    two_per_trip(att_tile)
    for h in range(N_HEADS_A):
        l = ml_ref[N_HEADS_A + h:N_HEADS_A + h + 1, :]
        o_ref[:, h * HEAD_DIM:(h + 1) * HEAD_DIM] = (acc_ref[h] / l).T.astype(o_ref.dtype)


def _dsa(qt, ka, vt, iqt, ik_ext, iwt):
    bsz, s_len, _ = ka.shape
    t = DSA_T
    n_t = s_len // t
    topk = min(IDX_TOPK_MAX, s_len // 4)

    def qtile(r):
        return pl.BlockSpec((None, None, r, t), lambda b, i: (b, i, 0, 0))

    return pl.pallas_call(
        functools.partial(_dsa_kernel, topk),
        grid=(bsz, n_t),
        in_specs=[qtile(A_WIDTH), qtile(IDX_HEADS * IDX_K), qtile(IDX_HEADS),
                  pl.BlockSpec((None, s_len, A_WIDTH), lambda b, i: (b, 0, 0)),
                  pl.BlockSpec((None, n_t, A_WIDTH, t), lambda b, i: (b, 0, 0, 0)),
                  pl.BlockSpec((None, s_len, IDX_K), lambda b, i: (b, 0, 0))],
        out_specs=pl.BlockSpec((None, t, A_WIDTH), lambda b, i: (b, i, 0)),
        out_shape=jax.ShapeDtypeStruct((bsz, s_len, A_WIDTH), BF16),
        scratch_shapes=[pltpu.VMEM((n_t, t, t), jnp.int32),
                        pltpu.VMEM((N_HEADS_A, HEAD_DIM, t), F32),
                        pltpu.VMEM((SUBLANES, t), jnp.int32),
                        pltpu.VMEM((SUBLANES, t), jnp.int32),
                        pltpu.VMEM((SUBLANES, t), F32),
                        pltpu.VMEM((2 * N_HEADS_A, t), F32)],
        compiler_params=_cparams(("parallel", "arbitrary")),
        name="dsa",
    )(qt, iqt, iwt, ka, vt, ik_ext)


GLA_ROWS = 256


GLA_BLK = SUBLANES
GLA_NBLK = CHUNK // GLA_BLK
GLA_PAIRS = [(i, j) for i in range(GLA_NBLK) for j in range(i)]


def _split3(x):
    hi = x.astype(BF16)
    r1 = x - hi.astype(F32)
    mid = r1.astype(BF16)
    return hi, mid, (r1 - mid.astype(F32)).astype(BF16)


def _gla_cumsum_matrix():
    t = np.arange(CHUNK)
    blk = t // GLA_BLK
    incl = t[None, :] <= t[:, None]
    start = t[None, :] < (GLA_BLK * blk)[:, None]
    end = t[None, :] < (GLA_BLK * (blk + 1))[:, None]
    bounds = t[None, :] < (GLA_BLK * np.arange(GLA_NBLK))[:, None]
    pad = np.zeros((GLA_BLK, CHUNK), bool)
    m = np.concatenate([incl, start, end, bounds, pad], axis=0).astype(np.float32)
    return jnp.asarray(np.concatenate([m, m, m], axis=1), BF16)


def _gla_kernel(q_ref, k_ref, v_ref, og_ref, gl_ref, wg_ref, bg_ref, gn_ref, cm_ref, sg_ref, o_ref,
                st_ref, lg_ref, b_ref, kc_ref):
    @pl.when(pl.program_id(1) == 0)
    def _():
        st_ref[...] = jnp.zeros_like(st_ref)

    g_hi, g_mid, g_lo = (p.astype(F32) for p in _split3(gl_ref[...]))
    r = GLA_GATE_RANK
    packed = (g_hi + pltpu.roll(g_hi, r, 1) + pltpu.roll(g_mid, 2 * r, 1) + pltpu.roll(g_hi, 3 * r, 1)
              + pltpu.roll(g_lo, 4 * r, 1) + pltpu.roll(g_mid, 5 * r, 1))
    gate = jnp.dot(packed.astype(BF16), wg_ref[...], preferred_element_type=F32) + bg_ref[...]
    lg_ref[...] = _log_sigmoid(gate) * (1.0 / GLA_TAU)

    sub = lax.broadcasted_iota(jnp.int32, (GLA_BLK, LANES), 0)
    lane_c = lax.broadcasted_iota(jnp.int32, (GLA_BLK, CHUNK), 1)
    pairs = [slice(p * LANES, (p + 1) * LANES) for p in range(N_HEADS_B * GLA_DK // LANES)]
    first = lax.broadcasted_iota(jnp.int32, (CHUNK, LANES), 1) < GLA_DK

    def chunk(ci, carry):
        rows = pl.ds(ci * CHUNK, CHUNK)
        cums = []
        for ps in pairs:
            parts = _split3(lg_ref[rows, ps])
            cums.append(jnp.dot(cm_ref[...], jnp.concatenate(parts, axis=0), preferred_element_type=F32))
        for p, ps in enumerate(pairs):
            b_ref[rows, ps] = cums[p][0:CHUNK]
        kc_ref[rows, :] = k_ref[rows, :].astype(F32)
        lhss, kbs, xalls, qes, kds, bls = [], [], [], [], [], []
        for p, ps in enumerate(pairs):
            b = cums[p][0:CHUNK]
            b_start = cums[p][CHUNK:2 * CHUNK]
            b_end = cums[p][2 * CHUNK:3 * CHUNK]
            bound = cums[p][3 * CHUNK:3 * CHUNK + GLA_NBLK]
            q = q_ref[rows, ps].astype(F32) * GLA_DK ** -0.5
            k = kc_ref[rows, ps]
            qb = q * jnp.exp(b - b_start)
            kb = k * jnp.exp(b_end - b)
            kbs.append((jnp.where(first, kb, 0.0).astype(BF16), jnp.where(first, 0.0, kb).astype(BF16)))
            dec = [jnp.exp(jnp.where(sub > j, bound - bound[j + 1:j + 2], -jnp.inf))
                   for j in range(GLA_NBLK - 1)]
            lhss.append(jnp.concatenate([qb[i * GLA_BLK:(i + 1) * GLA_BLK] * dec[j][i:i + 1]
                                         for i, j in GLA_PAIRS], axis=0).astype(BF16))
            xs = []
            for s in range(CHUNK):
                blk = slice(s // GLA_BLK * GLA_BLK, (s // GLA_BLK + 1) * GLA_BLK)
                srow = pl.ds(ci * CHUNK + s, 1)
                ks = jnp.broadcast_to(kc_ref[srow, ps], (GLA_BLK, LANES))
                bs = jnp.broadcast_to(b_ref[srow, ps], (GLA_BLK, LANES))
                xs.append(q[blk] * ks * jnp.exp(jnp.where(sub >= s % GLA_BLK, b[blk] - bs, -jnp.inf)))
            xalls.append(jnp.concatenate(xs, axis=0).astype(BF16))
            bl = b[CHUNK - 1:CHUNK]
            kd = k * jnp.exp(bl - b)
            qes += [(q * jnp.exp(b)).astype(BF16)] * 2
            kds += [jnp.where(first, kd, 0.0).astype(BF16), jnp.where(first, 0.0, kd).astype(BF16)]
            bls += [bl, bl]
        cross = [[_dot_t(lhss[p], kbs[p][e]) for e in range(2)] for p in range(len(pairs))]
        sums = [jnp.dot(xalls[p], sg_ref[...], preferred_element_type=F32) for p in range(len(pairs))]
        atts = []
        for p in range(len(pairs)):
            for e in range(2):
                att_rows = []
                for i in range(GLA_NBLK):
                    att = jnp.zeros((GLA_BLK, CHUNK), F32)
                    for g, (gi, j) in enumerate(GLA_PAIRS):
                        if gi == i:
                            att = jnp.where(lane_c // GLA_BLK == j,
                                            cross[p][e][g * GLA_BLK:(g + 1) * GLA_BLK], att)
                    for s in range(i * GLA_BLK, (i + 1) * GLA_BLK):
                        col = sums[p][s * GLA_BLK:(s + 1) * GLA_BLK, e * LANES:e * LANES + CHUNK]
                        att = jnp.where(lane_c == s, col, att)
                    att_rows.append(att)
                atts.append(jnp.concatenate(att_rows, axis=0).astype(BF16))
        for h in range(N_HEADS_B):
            hs = slice(h * GLA_DV, (h + 1) * GLA_DV)
            v = v_ref[rows, hs]
            st = st_ref[h]
            o = _dot_t(qes[h], st.astype(BF16)) + jnp.dot(atts[h], v, preferred_element_type=F32)
            st_ref[h] = st * jnp.exp(bls[h]) + lax.dot_general(
                v, kds[h], (((0,), (0,)), ((), ())), preferred_element_type=F32)
            og = og_ref[rows, hs].astype(F32)
            o_ref[rows, hs] = (_rms(o, gn_ref[...]) * (og * jax.nn.sigmoid(og))).astype(o_ref.dtype)
        return carry

    for ci in range(q_ref.shape[0] // CHUNK):
        chunk(ci, 0)


def _gla(qkvg, g_low, wg, bg, gn):
    bsz, s_len, _ = qkvg.shape

    def col(c, w):
        return pl.BlockSpec((None, GLA_ROWS, w), lambda b, s, c=c: (b, s, c))

    cm = _gla_cumsum_matrix()
    qk_w = N_HEADS_B * GLA_DK
    lane = np.arange(LANES)[:, None] // GLA_DK
    sg = jnp.asarray(lane == np.arange(2 * LANES)[None, :] // LANES, BF16)
    return pl.pallas_call(
        _gla_kernel,
        grid=(bsz, s_len // GLA_ROWS),
        in_specs=[col(0, qk_w), col(1, qk_w), col(1, W512), col(2, W512), col(0, LANES),
                  pl.BlockSpec((LANES, qk_w), lambda b, s: (0, 0)),
                  pl.BlockSpec((1, qk_w), lambda b, s: (0, 0)),
                  pl.BlockSpec((1, GLA_DV), lambda b, s: (0, 0)),
                  pl.BlockSpec(cm.shape, lambda b, s: (0, 0)),
                  pl.BlockSpec(sg.shape, lambda b, s: (0, 0))],
        out_specs=pl.BlockSpec((None, GLA_ROWS, B_WIDTH), lambda b, s: (b, s, 0)),
        out_shape=jax.ShapeDtypeStruct((bsz, s_len, B_WIDTH), BF16),
        scratch_shapes=[pltpu.VMEM((N_HEADS_B, GLA_DV, LANES), F32),
                        pltpu.VMEM((GLA_ROWS, qk_w), F32),
                        pltpu.VMEM((GLA_ROWS, qk_w), F32),
                        pltpu.VMEM((GLA_ROWS, qk_w), F32)],
        compiler_params=_cparams(("parallel", "arbitrary")),
        name="gla",
    )(qkvg, qkvg, qkvg, qkvg, g_low, wg, bg, gn.reshape(1, GLA_DV), cm, sg)


def _gla_gate_weights(w_gate_up):
    hi, mid, lo = _split3(w_gate_up)
    w = jnp.concatenate([hi, mid, hi, lo, hi, mid], axis=0)
    return jnp.pad(w, ((0, LANES - w.shape[0]), (0, 0)))


SB_T = 256
SB_HEADS = 2
SB_W = SB_HEADS * HEAD_DIM
LOG2E = 1.4426950408889634


def _sb_kernel(q_ref, kn_ref, v_ref, cw_ref, o_ref, acc_ref, car_ref, zn_ref):
    def tile(i, carry):
        _sb_tile(i, q_ref, kn_ref, v_ref, cw_ref, o_ref, acc_ref, car_ref, zn_ref)
        return carry

    lax.fori_loop(0, q_ref.shape[0] // SB_T, tile, 0)


def _sb_tile(i, q_ref, kn_ref, v_ref, cw_ref, o_ref, acc_ref, car_ref, zn_ref):
    heads = [slice(h * HEAD_DIM, (h + 1) * HEAD_DIM) for h in range(SB_HEADS)]
    q_rows = pl.ds(pl.multiple_of(i * SB_T, SB_T), SB_T)
    qs = [q_ref[q_rows, hs] for hs in heads]

    def logits(j):
        start = pl.multiple_of(j * SB_T, SB_T)
        return [_dot_t(qs[h], kn_ref[pl.ds(start, SB_T), hs]) for h, hs in enumerate(heads)]

    def block(j, zs, next_slot, diag):
        start = pl.multiple_of(j * SB_T, SB_T)
        if diag:
            strict = (lax.broadcasted_iota(jnp.int32, (SB_T, SB_T), 1)
                      < lax.broadcasted_iota(jnp.int32, (SB_T, SB_T), 0))
        for h, z in enumerate(logits(jnp.maximum(j - 1, 0))):
            zn_ref[next_slot * SB_HEADS + h] = z
        hls = []
        for z in zs:
            sp = jnp.maximum(z, 0.0) + jnp.log2(1.0 + jnp.exp2(-jnp.abs(z)))
            if diag:
                sp = jnp.where(strict, sp, 0.0)
            hls.append(sp.astype(BF16))
        rs = [jnp.dot(hl, cw_ref[...], preferred_element_type=F32) for hl in hls]
        avs = []
        for h in range(SB_HEADS):
            car = car_ref[h]
            a = jnp.exp2(zs[h] - rs[h] - jnp.concatenate([car] * (SB_T // LANES), axis=1))
            if diag:
                a = jnp.where(strict, a, 0.0)
            avs.append(a.astype(BF16))
            car_ref[h] = car + jnp.broadcast_to(rs[h][:, 0:1], car.shape)
        for h, hs in enumerate(heads):
            acc_ref[:, hs] += jnp.dot(avs[h], v_ref[pl.ds(start, SB_T), hs], preferred_element_type=F32)

    acc_ref[...] = jnp.zeros_like(acc_ref)
    car_ref[...] = jnp.zeros_like(car_ref)

    def trip(j, slot):
        zs = [zn_ref[slot * SB_HEADS + h] for h in range(SB_HEADS)]
        block(j, zs, 1 - slot, False)

    odd = i % 2

    @pl.when(odd == 0)
    def _():
        block(i, logits(i), 0, True)

    @pl.when(odd == 1)
    def _():
        block(i, logits(i), 1, True)
        trip(i - 1, 1)

    def body(u, carry):
        j = i - 1 - odd - 2 * u
        trip(j, 0)
        trip(j - 1, 1)
        return carry

    lax.fori_loop(0, i // 2, body, 0)
    o_ref[q_rows, :] = acc_ref[...].astype(o_ref.dtype)


def _stick_breaking(qkv):
    bsz, s_len, _ = qkv.shape
    cum_w = (jnp.arange(SB_T)[:, None] >= jnp.arange(SB_T)[None, :]).astype(BF16)
    n_grp = N_HEADS_C // SB_HEADS
    return pl.pallas_call(
        _sb_kernel,
        grid=(bsz, n_grp),
        in_specs=[pl.BlockSpec((None, s_len, SB_W), lambda b, h: (b, 0, h)),
                  pl.BlockSpec((None, s_len, SB_W), lambda b, h: (b, 0, n_grp + h)),
                  pl.BlockSpec((None, s_len, SB_W), lambda b, h: (b, 0, 2 * n_grp + h)),
                  pl.BlockSpec((SB_T, SB_T), lambda b, h: (0, 0))],
        out_specs=pl.BlockSpec((None, s_len, SB_W), lambda b, h: (b, 0, h)),
        out_shape=jax.ShapeDtypeStruct((bsz, s_len, C_WIDTH), BF16),
        scratch_shapes=[pltpu.VMEM((SB_T, SB_W), F32), pltpu.VMEM((SB_HEADS, SB_T, LANES), F32),
                        pltpu.VMEM((2 * SB_HEADS, SB_T, SB_T), F32)],
        compiler_params=_cparams(("parallel", "parallel")),
        name="stick_breaking",
    )(qkv, qkv, qkv, cum_w)


def _pad_heads(w, n_heads, width):
    lead = w.shape[:-1]
    d = w.shape[-1] // n_heads
    w = w.reshape(*lead, n_heads, d)
    w = jnp.pad(w, [(0, 0)] * len(lead) + [(0, 0), (0, width - d)])
    return w.reshape(*lead, n_heads * width)


def _pad_cols(w, width):
    return jnp.pad(w, [(0, 0)] * (w.ndim - 1) + [(0, width - w.shape[-1])])


def _layer0_weights(w_in):
    parts, o = [], 0
    for s in AB_SIZES:
        parts.append(w_in[:, o:o + s])
        o += s
    qa, ka, va, iq, ik, iw, qb, kb, vb, glow, og = parts
    cols = [qa, ka, va, qb, kb, vb, og,
            iq, _pad_cols(ik, LANES), _pad_cols(iw, LANES), _pad_cols(glow, 2 * LANES)]
    return jnp.concatenate(cols, axis=1).astype(BF16)


def kernel(x, g_mix, g_ffn, w_in_ab, gq_a, gk_a, w_gate_up, b_gate, g_gla, w_out_ab,
           w_in_c, gq_c, gk_c, w_out_c, w_up, w_down):
    bsz, s_len, d = x.shape
    m = bsz * s_len
    h = x.reshape(m, d)

    qk_gains = jnp.stack([gq_a[0] * HEAD_DIM ** -0.5, gk_a[0]])
    rope_a = _rope_tables(s_len, HEAD_DIM, 1)
    rope_i = _rope_tables(s_len, IDX_DIM, LANES // IDX_DIM)
    *dsa_ops, qkvg, g_low = _proj0(h, g_mix[0], _layer0_weights(w_in_ab[0]), qk_gains, rope_a, rope_i,
                                   bsz, s_len, tm=PROJ_TM)
    oa = _dsa(*dsa_ops)
    wg = _gla_gate_weights(w_gate_up[0])
    bg = b_gate[0].reshape(1, N_HEADS_B * GLA_DK)
    ob = _gla(qkvg, g_low, wg, bg, g_gla[0])
    w_o = w_out_ab[0].astype(BF16)
    h = _proj_mlp(h, [oa.reshape(m, A_WIDTH), ob.reshape(m, B_WIDTH)], [w_o[:A_WIDTH], w_o[A_WIDTH:]],
                  g_ffn[0], w_up[0].astype(BF16), w_down[0].astype(BF16), tm=MLP_TM, tf=MLP_TF)

    qk_gains = jnp.stack([gq_c[0] * (HEAD_DIM ** -0.5 * LOG2E), gk_c[0]])
    qkv = _norm_matmul(h, g_mix[1], w_in_c[0].astype(BF16), qk_gains, tm=PROJ_TM, tn=C_WIDTH)
    oc = _stick_breaking(qkv.reshape(bsz, s_len, 3 * C_WIDTH))
    h = _proj_mlp(h, [oc.reshape(m, C_WIDTH)], [w_out_c[0].astype(BF16)],
                  g_ffn[1], w_up[1].astype(BF16), w_down[1].astype(BF16), tm=MLP_TM, tf=MLP_TF)
    return h.reshape(bsz, s_len, d)
```

```python
import functools

import numpy as np
import jax
import jax.numpy as jnp
from jax import lax
from jax.experimental import pallas as pl
from jax.experimental.pallas import tpu as pltpu

F32 = jnp.float32
BF16 = jnp.bfloat16

D_MODEL = 1024
CHUNK = 64
HEAD_DIM = 128
EPS = 1e-6
N_HEADS_A = 4
A_WIDTH = N_HEADS_A * HEAD_DIM
IDX_HEADS = 8
IDX_DIM = 64
IDX_TOPK_MAX = 256
N_HEADS_B = 4
GLA_DK = 64
GLA_DV = 128
GLA_GATE_RANK = 16
GLA_TAU = 16.0
B_WIDTH = N_HEADS_B * GLA_DV
N_HEADS_C = 8
C_WIDTH = N_HEADS_C * HEAD_DIM
ROPE_THETA = 500000.0
ROPE_FRACTION = 4
D_FF = 4 * D_MODEL
AB_SIZES = (A_WIDTH, A_WIDTH, A_WIDTH, IDX_HEADS * IDX_DIM, IDX_DIM, IDX_HEADS,
            N_HEADS_B * GLA_DK, N_HEADS_B * GLA_DK, B_WIDTH, GLA_GATE_RANK, B_WIDTH)

LANES = 128
VMEM_LIMIT = 48 * 1024 * 1024
MLP_TM, MLP_TF = 512, 512
PROJ_TM = 512

W512 = 512
P_QA, P_KA, P_VA, P_QKB, P_VB, P_OG, P_IQ, P_MISC = range(8)

INT_MIN = np.int32(-2 ** 31)


def _cparams(sem):
    return pltpu.CompilerParams(dimension_semantics=sem, vmem_limit_bytes=VMEM_LIMIT)


def _rms(x, g):
    return x * lax.rsqrt(jnp.mean(x * x, axis=-1, keepdims=True) + EPS) * g


def _log_sigmoid(x):
    return jnp.minimum(x, 0.0) - jnp.log1p(jnp.exp(-jnp.abs(x)))


def _dot_t(a, b):
    return lax.dot_general(a, b, (((1,), (1,)), ((), ())), preferred_element_type=F32)


def _norm_matmul_kernel(tn, n_qk, x_ref, g_ref, w_ref, hg_ref, o_ref):
    xn = _rms(x_ref[...], g_ref[...]).astype(BF16)
    for c in range(w_ref.shape[1] // tn):
        y = jnp.dot(xn, w_ref[:, c * tn:(c + 1) * tn], preferred_element_type=F32)
        if c < n_qk:
            y = _head_rms(y, hg_ref[c])
        o_ref[:, c * tn:(c + 1) * tn] = y.astype(BF16)


def _norm_matmul(x, g, w, head_gains, tm, tn):
    m, d = x.shape
    n_qk = head_gains.shape[0]
    n = w.shape[1]
    return pl.pallas_call(
        functools.partial(_norm_matmul_kernel, tn, n_qk),
        grid=(m // tm,),
        in_specs=[pl.BlockSpec((tm, d), lambda i: (i, 0)),
                  pl.BlockSpec((1, d), lambda i: (0, 0)),
                  pl.BlockSpec((d, n), lambda i: (0, 0)),
                  pl.BlockSpec((n_qk, 1, HEAD_DIM), lambda i: (0, 0, 0))],
        out_specs=pl.BlockSpec((tm, n), lambda i: (i, 0)),
        out_shape=jax.ShapeDtypeStruct((m, n), BF16),
        compiler_params=_cparams(("parallel",)),
        name="norm_matmul",
    )(x, g.reshape(1, d), w, head_gains.reshape(n_qk, 1, HEAD_DIM))


def _proj_mlp_kernel(n_a, tf, h_ref, *refs):
    a_refs = refs[:n_a]
    wo_refs = refs[n_a:2 * n_a]
    g_ref, wu_ref, wd_ref, o_ref, xn_ref = refs[2 * n_a:]
    h1 = h_ref[...]
    for a_ref, wo_ref in zip(a_refs, wo_refs):
        h1 = h1 + jnp.dot(a_ref[...], wo_ref[...], preferred_element_type=F32)
    o_ref[...] = h1
    xn_ref[...] = _rms(h1, g_ref[...]).astype(BF16)
    for c in range(wu_ref.shape[1] // tf):
        u = jnp.dot(xn_ref[...], wu_ref[:, c * tf:(c + 1) * tf], preferred_element_type=F32)
        u = jnp.square(jnp.maximum(u, 0.0)).astype(BF16)
        o_ref[...] += jnp.dot(u, wd_ref[c * tf:(c + 1) * tf, :], preferred_element_type=F32)


def _proj_mlp(h, a_list, wo_list, g, w_up, w_down, tm, tf):
    m, d = h.shape
    n_a = len(a_list)

    def resident(w):
        return pl.BlockSpec(w.shape, lambda i: (0, 0))

    in_specs = [pl.BlockSpec((tm, d), lambda i: (i, 0))]
    in_specs += [pl.BlockSpec((tm, a.shape[1]), lambda i: (i, 0)) for a in a_list]
    in_specs += [resident(w) for w in wo_list]
    in_specs += [pl.BlockSpec((1, d), lambda i: (0, 0)), resident(w_up), resident(w_down)]
    return pl.pallas_call(
        functools.partial(_proj_mlp_kernel, n_a, tf),
        grid=(m // tm,),
        in_specs=in_specs,
        out_specs=pl.BlockSpec((tm, d), lambda i: (i, 0)),
        out_shape=jax.ShapeDtypeStruct((m, d), F32),
        scratch_shapes=[pltpu.VMEM((tm, d), BF16)],
        compiler_params=_cparams(("parallel",)),
        name="proj_mlp",
    )(h, *a_list, *wo_list, g.reshape(1, d), w_up, w_down)


def _rope(x, rot_ref, tab_ref):
    out = []
    for s in range(x.shape[1] // LANES):
        xs = x[:, s * LANES:(s + 1) * LANES]
        if xs.dtype == BF16:
            r = jnp.dot(xs, rot_ref[...], preferred_element_type=F32)
        else:
            hi, lo = _hi_lo(xs)
            r = (jnp.dot(hi, rot_ref[...], preferred_element_type=F32)
                 + jnp.dot(lo, rot_ref[...], preferred_element_type=F32))
        out.append(xs.astype(F32) * tab_ref[0] + r * tab_ref[1])
    return out


def _head_rms(x, g):
    out = []
    for s in range(x.shape[1] // HEAD_DIM):
        out.append(_rms(x[:, s * HEAD_DIM:(s + 1) * HEAD_DIM], g))
    return jnp.concatenate(out, axis=1)


DSA_T = 256
IDX_K = 4 * IDX_DIM


def _hi_lo(x):
    hi = x.astype(BF16)
    return hi, (x - hi.astype(F32)).astype(BF16)


def _proj0_kernel(x_ref, g_ref, w_ref, hg_ref, ra_ref, ri_ref, ta_ref, ti_ref,
                  qt_ref, ko_ref, vt_ref, iqt_ref, iko_ref, iwt_ref, gla_ref, gl_ref):
    n_sub = x_ref.shape[0] // DSA_T
    xn = _rms(x_ref[...], g_ref[...]).astype(BF16)

    def col(c):
        return jnp.dot(xn, w_ref[:, c * W512:(c + 1) * W512], preferred_element_type=F32)

    def sub_t(a, s):
        return a[s * DSA_T:(s + 1) * DSA_T].T

    q = jnp.concatenate(_rope(_head_rms(col(P_QA), hg_ref[0]).astype(BF16), ra_ref, ta_ref), axis=1)
    for s in range(n_sub):
        qt_ref[s] = sub_t(q, s).astype(BF16)
    k = _head_rms(col(P_KA), hg_ref[1]).astype(BF16)
    ko_ref[...] = jnp.concatenate(_rope(k, ra_ref, ta_ref), axis=1).astype(BF16)
    v = col(P_VA)
    for s in range(n_sub):
        vt_ref[s] = sub_t(v, s).astype(BF16)
    for c in (P_QKB, P_VB, P_OG):
        gla_ref[:, (c - P_QKB) * W512:(c - P_QKB + 1) * W512] = col(c).astype(BF16)
    iq = jnp.concatenate(_rope(col(P_IQ), ri_ref, ti_ref), axis=1) * IDX_DIM ** -0.5
    n_iq = IDX_HEADS * IDX_DIM
    for s in range(n_sub):
        hi, lo = _hi_lo(sub_t(iq, s))
        iqt_ref[s, 0:n_iq, :] = hi
        iqt_ref[s, n_iq:2 * n_iq, :] = lo
    misc = col(P_MISC)
    ik = _rope(misc[:, 0:LANES], ri_ref, ti_ref)[0]
    ik_hi, ik_lo = _hi_lo(ik)
    ika = ik_hi.astype(F32) + pltpu.roll(ik_lo.astype(F32), IDX_DIM, 1)
    iko_ref[...] = jnp.concatenate([ika, ika], axis=1).astype(BF16)
    iw = misc[:, LANES:2 * LANES] * IDX_HEADS ** -0.5
    for s in range(n_sub):
        iwt_ref[s] = sub_t(iw, s)[0:IDX_HEADS, :]
    gl_ref[...] = misc[:, 2 * LANES:3 * LANES]


def _proj0(x, g, w, head_gains, rope_a, rope_i, bsz, s_len, tm):
    m, d = x.shape
    n_t = s_len // DSA_T
    spb = s_len // tm
    n_sub = tm // DSA_T

    def rows_out(w):
        return pl.BlockSpec((None, tm, w), lambda i: (i // spb, i % spb, 0))

    def tile_out(r):
        return pl.BlockSpec((None, n_sub, r, DSA_T), lambda i: (i // spb, i % spb, 0, 0))

    tab_spec = pl.BlockSpec((2, tm, LANES), lambda i: (0, i % spb, 0))
    rot_spec = pl.BlockSpec((LANES, LANES), lambda i: (0, 0))
    sd = jax.ShapeDtypeStruct
    return pl.pallas_call(
        _proj0_kernel,
        grid=(m // tm,),
        in_specs=[pl.BlockSpec((tm, d), lambda i: (i, 0)),
                  pl.BlockSpec((1, d), lambda i: (0, 0)),
                  pl.BlockSpec(w.shape, lambda i: (0, 0)),
                  pl.BlockSpec((2, 1, HEAD_DIM), lambda i: (0, 0, 0)),
                  rot_spec, rot_spec, tab_spec, tab_spec],
        out_specs=[tile_out(A_WIDTH), rows_out(A_WIDTH), tile_out(A_WIDTH), tile_out(2 * IDX_HEADS * IDX_DIM),
                   rows_out(IDX_K), tile_out(IDX_HEADS), rows_out(3 * W512), rows_out(LANES)],
        out_shape=[sd((bsz, n_t, A_WIDTH, DSA_T), BF16), sd((bsz, s_len, A_WIDTH), BF16),
                   sd((bsz, n_t, A_WIDTH, DSA_T), BF16), sd((bsz, n_t, 2 * IDX_HEADS * IDX_DIM, DSA_T), BF16),
                   sd((bsz, s_len, IDX_K), BF16), sd((bsz, n_t, IDX_HEADS, DSA_T), F32),
                   sd((bsz, s_len, 3 * W512), BF16), sd((bsz, s_len, LANES), F32)],
        compiler_params=_cparams(("parallel",)),
        name="proj0",
    )(x, g.reshape(1, d), w, head_gains.reshape(2, 1, HEAD_DIM), rope_a[0], rope_i[0], rope_a[1], rope_i[1])


def _rope_tables(s_len, period, reps):
    rot_dims = period // ROPE_FRACTION
    half = rot_dims // 2
    inv = jnp.power(ROPE_THETA, -jnp.arange(half, dtype=F32) * 2.0 / rot_dims)
    ang = jnp.arange(s_len).astype(F32)[:, None] * inv[None, :]
    cos, sin = jnp.cos(ang), jnp.sin(ang)
    pad = jnp.zeros((s_len, period - rot_dims), F32)
    c = jnp.concatenate([cos, cos, pad + 1.0], axis=1)
    s = jnp.concatenate([sin, sin, pad], axis=1)
    lane = np.arange(LANES)
    pos = lane % period
    src, dst = lane[:, None], lane[None, :]
    rot = (np.where((src == dst - half) & (pos[None, :] >= half) & (pos[None, :] < rot_dims), 1.0, 0.0)
           - np.where((src == dst + half) & (pos[None, :] < half), 1.0, 0.0))
    return jnp.asarray(rot, BF16), jnp.stack([jnp.tile(t, (1, reps)) for t in (c, s)])


INT_MAX = np.int32(2 ** 31 - 1)
MASKED = -1e30
SUBLANES = 8


def _dsa_kernel(topk, qt_ref, iqt_ref, iwt_ref, k_ref, vt_ref, ik_ref, o_ref,
                key_ref, acc_ref, jb_ref, v_ref, nge_ref, ml_ref, iqx_ref):
    t = DSA_T
    qi = pl.program_id(1)
    n_kt = qi + 1
    s_len = k_ref.shape[0]
    qpos = lax.broadcasted_iota(jnp.int32, (1, t), 1)
    limit = qi * t + (qpos // CHUNK + 1) * CHUNK
    krow = lax.broadcasted_iota(jnp.int32, (t, t), 0)

    n_iq = IDX_HEADS * IDX_DIM
    for h in range(IDX_HEADS):
        hi = iqt_ref[h * IDX_DIM:(h + 1) * IDX_DIM, :]
        lo = iqt_ref[n_iq + h * IDX_DIM:n_iq + (h + 1) * IDX_DIM, :]
        for part, val in enumerate((hi, hi, lo, lo)):
            base = h * IDX_K + part * IDX_DIM
            iqx_ref[base:base + IDX_DIM, :] = val

    def two_per_trip(fn):
        def body(u, carry):
            fn(2 * u)
            fn(2 * u + 1)
            return carry

        lax.fori_loop(0, n_kt // 2, body, 0)

        @pl.when(n_kt % 2 == 1)
        def _():
            fn(n_kt - 1)

    def score_tile(kt):
        start = pl.multiple_of(kt * t, t)
        ikt = ik_ref[pl.ds(start, t), :]
        logits = [jnp.dot(ikt, iqx_ref[h * IDX_K:(h + 1) * IDX_K, :], preferred_element_type=F32)
                  for h in range(IDX_HEADS)]
        score = jnp.zeros((t, t), F32)
        for h in range(IDX_HEADS):
            score = score + jnp.maximum(logits[h], 0.0) * iwt_ref[h:h + 1, :]
        bits = pltpu.bitcast(score, jnp.int32)
        key = bits ^ ((bits >> 31) & jnp.int32(0x7FFFFFFF))
        key_ref[kt] = jnp.where(krow + start < limit, key, INT_MIN)

    two_per_trip(score_tile)

    def count(pred, n=None):
        def body(kt, acc):
            ones = jnp.where(pred(kt), 1.0, 0.0)
            return acc + jnp.sum(ones.reshape(t // SUBLANES, SUBLANES, t), axis=0)
        acc = jnp.zeros((SUBLANES, t), F32)
        if n is None:
            acc = lax.fori_loop(0, n_kt, body, acc)
        else:
            for kt in range(n):
                acc = body(kt, acc)
        return jnp.sum(acc, axis=0, keepdims=True)

    for n in range(1, key_ref.shape[0] + 1):
        @pl.when(n_kt == n)
        def _(n=n):
            def vbody(it, carry):
                lo, n_lo = carry
                cand = lo + lax.shift_left(jnp.int32(1), 31 - it)
                cnt = count(lambda kt: key_ref[kt] >= cand, n)
                ok = cnt >= topk
                return jnp.where(ok, cand, lo), jnp.where(ok, cnt, n_lo)

            lo, n_lo = lax.fori_loop(0, 32, vbody, (jnp.full((1, t), INT_MIN, jnp.int32),
                                                    jnp.full((1, t), float(n * t), F32)))
            v_ref[...] = jnp.broadcast_to(lo, v_ref.shape)
            nge_ref[...] = jnp.broadcast_to(n_lo, nge_ref.shape)

    v = v_ref[0:1, :]
    n_ge = nge_ref[0:1, :]

    jb_ref[...] = jnp.full(jb_ref.shape, INT_MAX, jnp.int32)

    @pl.when(jnp.max(n_ge) > topk)
    def _():
        need = topk - count(lambda kt: key_ref[kt] > v)
        nbits = (s_len - 1).bit_length()

        def jbody(it, jb):
            cand = jb + lax.shift_left(jnp.int32(1), nbits - 1 - it)
            few = count(lambda kt: (key_ref[kt] == v) & (krow + kt * t < cand)) < need
            return jnp.where(few, cand, jb)

        jb = lax.fori_loop(0, nbits, jbody, jnp.zeros((1, t), jnp.int32))
        jb_ref[...] = jnp.broadcast_to(jb, jb_ref.shape)

    jb = jnp.where(v == INT_MIN, -1, jb_ref[0:1, :])

    acc_ref[...] = jnp.zeros_like(acc_ref)

    def att_tile(kt):
        ms = [ml_ref[h:h + 1, :] for h in range(N_HEADS_A)]
        ls = [ml_ref[N_HEADS_A + h:N_HEADS_A + h + 1, :] for h in range(N_HEADS_A)]
        start = pl.multiple_of(kt * t, t)
        key = key_ref[kt]
        tie_bias = jnp.where(key == v, jnp.where(krow + start <= jb, 0.0, MASKED), MASKED)
        bias = jnp.where(key > v, 0.0, tie_bias)
        heads = [slice(h * HEAD_DIM, (h + 1) * HEAD_DIM) for h in range(N_HEADS_A)]
        ss = [jnp.dot(k_ref[pl.ds(start, t), hs], qt_ref[hs, :], preferred_element_type=F32) + bias
              for hs in heads]
        new_ms = [jnp.maximum(ms[h], jnp.max(ss[h], axis=0, keepdims=True)) for h in range(N_HEADS_A)]
        ps = [jnp.exp(ss[h] - new_ms[h]) for h in range(N_HEADS_A)]
        alphas = [jnp.exp(ms[h] - new_ms[h]) for h in range(N_HEADS_A)]
        new_ls = [alphas[h] * ls[h] + jnp.sum(ps[h], axis=0, keepdims=True) for h in range(N_HEADS_A)]
        pvs = [jnp.dot(vt_ref[kt, hs, :], ps[h].astype(BF16), preferred_element_type=F32)
               for h, hs in enumerate(heads)]
        for h in range(N_HEADS_A):
            acc_ref[h] = alphas[h] * acc_ref[h] + pvs[h]
            ml_ref[h:h + 1, :] = new_ms[h]
            ml_ref[N_HEADS_A + h:N_HEADS_A + h + 1, :] = new_ls[h]

    ml_ref[0:N_HEADS_A, :] = jnp.full((N_HEADS_A, t), MASKED, F32)
    ml_ref[N_HEADS_A:2 * N_HEADS_A, :] = jnp.zeros((N_HEADS_A, t), F32)
    two_per_trip(att_tile)
    for h in range(N_HEADS_A):
        l = ml_ref[N_HEADS_A + h:N_HEADS_A + h + 1, :]
        o_ref[:, h * HEAD_DIM:(h + 1) * HEAD_DIM] = (acc_ref[h] / l).T.astype(o_ref.dtype)


def _dsa(qt, ka, vt, iqt, ik_ext, iwt):
    bsz, s_len, _ = ka.shape
    t = DSA_T
    n_t = s_len // t
    topk = min(IDX_TOPK_MAX, s_len // 4)

    def qtile(r):
        return pl.BlockSpec((None, None, r, t), lambda b, i: (b, i, 0, 0))

    return pl.pallas_call(
        functools.partial(_dsa_kernel, topk),
        grid=(bsz, n_t),
        in_specs=[qtile(A_WIDTH), qtile(2 * IDX_HEADS * IDX_DIM), qtile(IDX_HEADS),
                  pl.BlockSpec((None, s_len, A_WIDTH), lambda b, i: (b, 0, 0)),
                  pl.BlockSpec((None, n_t, A_WIDTH, t), lambda b, i: (b, 0, 0, 0)),
                  pl.BlockSpec((None, s_len, IDX_K), lambda b, i: (b, 0, 0))],
        out_specs=pl.BlockSpec((None, t, A_WIDTH), lambda b, i: (b, i, 0)),
        out_shape=jax.ShapeDtypeStruct((bsz, s_len, A_WIDTH), BF16),
        scratch_shapes=[pltpu.VMEM((n_t, t, t), jnp.int32),
                        pltpu.VMEM((N_HEADS_A, HEAD_DIM, t), F32),
                        pltpu.VMEM((SUBLANES, t), jnp.int32),
                        pltpu.VMEM((SUBLANES, t), jnp.int32),
                        pltpu.VMEM((SUBLANES, t), F32),
                        pltpu.VMEM((2 * N_HEADS_A, t), F32),
                        pltpu.VMEM((IDX_HEADS * IDX_K, t), BF16)],
        compiler_params=_cparams(("parallel", "arbitrary")),
        name="dsa",
    )(qt, iqt, iwt, ka, vt, ik_ext)


GLA_ROWS = 256


GLA_BLK = SUBLANES
GLA_NBLK = CHUNK // GLA_BLK
GLA_PAIRS = [(i, j) for i in range(GLA_NBLK) for j in range(i)]


def _split3(x):
    hi = x.astype(BF16)
    r1 = x - hi.astype(F32)
    mid = r1.astype(BF16)
    return hi, mid, (r1 - mid.astype(F32)).astype(BF16)


def _gla_cumsum_matrix():
    t = np.arange(CHUNK)
    blk = t // GLA_BLK
    incl = t[None, :] <= t[:, None]
    start = t[None, :] < (GLA_BLK * blk)[:, None]
    end = t[None, :] < (GLA_BLK * (blk + 1))[:, None]
    bounds = t[None, :] < (GLA_BLK * np.arange(GLA_NBLK))[:, None]
    pad = np.zeros((GLA_BLK, CHUNK), bool)
    m = np.concatenate([incl, start, end, bounds, pad], axis=0).astype(np.float32)
    return jnp.asarray(np.concatenate([m, m, m], axis=1), BF16)


def _gla_kernel(q_ref, k_ref, v_ref, og_ref, gl_ref, wg_ref, bg_ref, gn_ref, cm_ref, sg_ref, o_ref,
                st_ref, lg_ref, b_ref, kc_ref):
    @pl.when(pl.program_id(1) == 0)
    def _():
        st_ref[...] = jnp.zeros_like(st_ref)

    g_hi, g_mid, g_lo = (p.astype(F32) for p in _split3(gl_ref[...]))
    r = GLA_GATE_RANK
    packed = (g_hi + pltpu.roll(g_hi, r, 1) + pltpu.roll(g_mid, 2 * r, 1) + pltpu.roll(g_hi, 3 * r, 1)
              + pltpu.roll(g_lo, 4 * r, 1) + pltpu.roll(g_mid, 5 * r, 1))
    gate = jnp.dot(packed.astype(BF16), wg_ref[...], preferred_element_type=F32) + bg_ref[...]
    lg_ref[...] = _log_sigmoid(gate) * (1.0 / GLA_TAU)

    sub = lax.broadcasted_iota(jnp.int32, (GLA_BLK, LANES), 0)
    lane_c = lax.broadcasted_iota(jnp.int32, (GLA_BLK, CHUNK), 1)
    pairs = [slice(p * LANES, (p + 1) * LANES) for p in range(N_HEADS_B * GLA_DK // LANES)]
    first = lax.broadcasted_iota(jnp.int32, (CHUNK, LANES), 1) < GLA_DK

    def chunk(ci, carry):
        rows = pl.ds(ci * CHUNK, CHUNK)
        cums = []
        for ps in pairs:
            parts = _split3(lg_ref[rows, ps])
            cums.append(jnp.dot(cm_ref[...], jnp.concatenate(parts, axis=0), preferred_element_type=F32))
        for p, ps in enumerate(pairs):
            b_ref[rows, ps] = cums[p][0:CHUNK]
        kc_ref[rows, :] = k_ref[rows, :].astype(F32)
        lhss, kbs, xalls, qes, kds, bls = [], [], [], [], [], []
        for p, ps in enumerate(pairs):
            b = cums[p][0:CHUNK]
            b_start = cums[p][CHUNK:2 * CHUNK]
            b_end = cums[p][2 * CHUNK:3 * CHUNK]
            bound = cums[p][3 * CHUNK:3 * CHUNK + GLA_NBLK]
            q = q_ref[rows, ps].astype(F32) * GLA_DK ** -0.5
            k = kc_ref[rows, ps]
            qb = q * jnp.exp(b - b_start)
            kb = k * jnp.exp(b_end - b)
            kbs.append((jnp.where(first, kb, 0.0).astype(BF16), jnp.where(first, 0.0, kb).astype(BF16)))
            dec = [jnp.exp(jnp.where(sub > j, bound - bound[j + 1:j + 2], -jnp.inf))
                   for j in range(GLA_NBLK - 1)]
            lhss.append(jnp.concatenate([qb[i * GLA_BLK:(i + 1) * GLA_BLK] * dec[j][i:i + 1]
                                         for i, j in GLA_PAIRS], axis=0).astype(BF16))
            xs = []
            for s in range(CHUNK):
                blk = slice(s // GLA_BLK * GLA_BLK, (s // GLA_BLK + 1) * GLA_BLK)
                srow = pl.ds(ci * CHUNK + s, 1)
                ks = jnp.broadcast_to(kc_ref[srow, ps], (GLA_BLK, LANES))
                bs = jnp.broadcast_to(b_ref[srow, ps], (GLA_BLK, LANES))
                xs.append(q[blk] * ks * jnp.exp(jnp.where(sub >= s % GLA_BLK, b[blk] - bs, -jnp.inf)))
            xalls.append(jnp.concatenate(xs, axis=0).astype(BF16))
            bl = b[CHUNK - 1:CHUNK]
            kd = k * jnp.exp(bl - b)
            qes += [(q * jnp.exp(b)).astype(BF16)] * 2
            kds += [jnp.where(first, kd, 0.0).astype(BF16), jnp.where(first, 0.0, kd).astype(BF16)]
            bls += [bl, bl]
        cross = [[_dot_t(lhss[p], kbs[p][e]) for e in range(2)] for p in range(len(pairs))]
        sums = [jnp.dot(xalls[p], sg_ref[...], preferred_element_type=F32) for p in range(len(pairs))]
        atts = []
        for p in range(len(pairs)):
            for e in range(2):
                att_rows = []
                for i in range(GLA_NBLK):
                    att = jnp.zeros((GLA_BLK, CHUNK), F32)
                    for g, (gi, j) in enumerate(GLA_PAIRS):
                        if gi == i:
                            att = jnp.where(lane_c // GLA_BLK == j,
                                            cross[p][e][g * GLA_BLK:(g + 1) * GLA_BLK], att)
                    for s in range(i * GLA_BLK, (i + 1) * GLA_BLK):
                        col = sums[p][s * GLA_BLK:(s + 1) * GLA_BLK, e * LANES:e * LANES + CHUNK]
                        att = jnp.where(lane_c == s, col, att)
                    att_rows.append(att)
                atts.append(jnp.concatenate(att_rows, axis=0).astype(BF16))
        for h in range(N_HEADS_B):
            hs = slice(h * GLA_DV, (h + 1) * GLA_DV)
            v = v_ref[rows, hs]
            st = st_ref[h]
            o = _dot_t(qes[h], st.astype(BF16)) + jnp.dot(atts[h], v, preferred_element_type=F32)
            st_ref[h] = st * jnp.exp(bls[h]) + lax.dot_general(
                v, kds[h], (((0,), (0,)), ((), ())), preferred_element_type=F32)
            og = og_ref[rows, hs].astype(F32)
            o_ref[rows, hs] = (_rms(o, gn_ref[...]) * (og * jax.nn.sigmoid(og))).astype(o_ref.dtype)
        return carry

    for ci in range(q_ref.shape[0] // CHUNK):
        chunk(ci, 0)


def _gla(qkvg, g_low, wg, bg, gn):
    bsz, s_len, _ = qkvg.shape

    def col(c, w):
        return pl.BlockSpec((None, GLA_ROWS, w), lambda b, s, c=c: (b, s, c))

    cm = _gla_cumsum_matrix()
    qk_w = N_HEADS_B * GLA_DK
    lane = np.arange(LANES)[:, None] // GLA_DK
    sg = jnp.asarray(lane == np.arange(2 * LANES)[None, :] // LANES, BF16)
    return pl.pallas_call(
        _gla_kernel,
        grid=(bsz, s_len // GLA_ROWS),
        in_specs=[col(0, qk_w), col(1, qk_w), col(1, W512), col(2, W512), col(0, LANES),
                  pl.BlockSpec((LANES, qk_w), lambda b, s: (0, 0)),
                  pl.BlockSpec((1, qk_w), lambda b, s: (0, 0)),
                  pl.BlockSpec((1, GLA_DV), lambda b, s: (0, 0)),
                  pl.BlockSpec(cm.shape, lambda b, s: (0, 0)),
                  pl.BlockSpec(sg.shape, lambda b, s: (0, 0))],
        out_specs=pl.BlockSpec((None, GLA_ROWS, B_WIDTH), lambda b, s: (b, s, 0)),
        out_shape=jax.ShapeDtypeStruct((bsz, s_len, B_WIDTH), BF16),
        scratch_shapes=[pltpu.VMEM((N_HEADS_B, GLA_DV, LANES), F32),
                        pltpu.VMEM((GLA_ROWS, qk_w), F32),
                        pltpu.VMEM((GLA_ROWS, qk_w), F32),
                        pltpu.VMEM((GLA_ROWS, qk_w), F32)],
        compiler_params=_cparams(("parallel", "arbitrary")),
        name="gla",
    )(qkvg, qkvg, qkvg, qkvg, g_low, wg, bg, gn.reshape(1, GLA_DV), cm, sg)


def _gla_gate_weights(w_gate_up):
    hi, mid, lo = _split3(w_gate_up)
    w = jnp.concatenate([hi, mid, hi, lo, hi, mid], axis=0)
    return jnp.pad(w, ((0, LANES - w.shape[0]), (0, 0)))


SB_T = 256
SB_HEADS = 2
SB_W = SB_HEADS * HEAD_DIM
LOG2E = 1.4426950408889634


def _sb_kernel(q_ref, kn_ref, v_ref, cw_ref, o_ref, acc_ref, car_ref, zn_ref):
    def tile(i, carry):
        _sb_tile(i, q_ref, kn_ref, v_ref, cw_ref, o_ref, acc_ref, car_ref, zn_ref)
        return carry

    lax.fori_loop(0, q_ref.shape[0] // SB_T, tile, 0)


def _sb_tile(i, q_ref, kn_ref, v_ref, cw_ref, o_ref, acc_ref, car_ref, zn_ref):
    heads = [slice(h * HEAD_DIM, (h + 1) * HEAD_DIM) for h in range(SB_HEADS)]
    q_rows = pl.ds(pl.multiple_of(i * SB_T, SB_T), SB_T)
    qs = [q_ref[q_rows, hs] for hs in heads]

    def logits(j):
        start = pl.multiple_of(j * SB_T, SB_T)
        return [_dot_t(qs[h], kn_ref[pl.ds(start, SB_T), hs]) for h, hs in enumerate(heads)]

    def block(j, zs, next_slot, diag):
        start = pl.multiple_of(j * SB_T, SB_T)
        if diag:
            strict = (lax.broadcasted_iota(jnp.int32, (SB_T, SB_T), 1)
                      < lax.broadcasted_iota(jnp.int32, (SB_T, SB_T), 0))
        for h, z in enumerate(logits(jnp.maximum(j - 1, 0))):
            zn_ref[next_slot * SB_HEADS + h] = z
        hls = []
        for z in zs:
            sp = jnp.maximum(z, 0.0) + jnp.log2(1.0 + jnp.exp2(-jnp.abs(z)))
            if diag:
                sp = jnp.where(strict, sp, 0.0)
            hls.append(sp.astype(BF16))
        rs = [jnp.dot(hl, cw_ref[...], preferred_element_type=F32) for hl in hls]
        avs = []
        for h in range(SB_HEADS):
            car = car_ref[h]
            a = jnp.exp2(zs[h] - rs[h] - jnp.concatenate([car] * (SB_T // LANES), axis=1))
            if diag:
                a = jnp.where(strict, a, 0.0)
            avs.append(a.astype(BF16))
            car_ref[h] = car + jnp.broadcast_to(rs[h][:, 0:1], car.shape)
        for h, hs in enumerate(heads):
            acc_ref[:, hs] += jnp.dot(avs[h], v_ref[pl.ds(start, SB_T), hs], preferred_element_type=F32)

    acc_ref[...] = jnp.zeros_like(acc_ref)
    car_ref[...] = jnp.zeros_like(car_ref)

    def trip(j, slot):
        zs = [zn_ref[slot * SB_HEADS + h] for h in range(SB_HEADS)]
        block(j, zs, 1 - slot, False)

    odd = i % 2

    @pl.when(odd == 0)
    def _():
        block(i, logits(i), 0, True)

    @pl.when(odd == 1)
    def _():
        block(i, logits(i), 1, True)
        trip(i - 1, 1)

    def body(u, carry):
        j = i - 1 - odd - 2 * u
        trip(j, 0)
        trip(j - 1, 1)
        return carry

    lax.fori_loop(0, i // 2, body, 0)
    o_ref[q_rows, :] = acc_ref[...].astype(o_ref.dtype)


def _stick_breaking(qkv):
    bsz, s_len, _ = qkv.shape
    cum_w = (jnp.arange(SB_T)[:, None] >= jnp.arange(SB_T)[None, :]).astype(BF16)
    n_grp = N_HEADS_C // SB_HEADS
    return pl.pallas_call(
        _sb_kernel,
        grid=(bsz, n_grp),
        in_specs=[pl.BlockSpec((None, s_len, SB_W), lambda b, h: (b, 0, h)),
                  pl.BlockSpec((None, s_len, SB_W), lambda b, h: (b, 0, n_grp + h)),
                  pl.BlockSpec((None, s_len, SB_W), lambda b, h: (b, 0, 2 * n_grp + h)),
                  pl.BlockSpec((SB_T, SB_T), lambda b, h: (0, 0))],
        out_specs=pl.BlockSpec((None, s_len, SB_W), lambda b, h: (b, 0, h)),
        out_shape=jax.ShapeDtypeStruct((bsz, s_len, C_WIDTH), BF16),
        scratch_shapes=[pltpu.VMEM((SB_T, SB_W), F32), pltpu.VMEM((SB_HEADS, SB_T, LANES), F32),
                        pltpu.VMEM((2 * SB_HEADS, SB_T, SB_T), F32)],
        compiler_params=_cparams(("parallel", "parallel")),
        name="stick_breaking",
    )(qkv, qkv, qkv, cum_w)


def _pad_cols(w, width):
    return jnp.pad(w, [(0, 0)] * (w.ndim - 1) + [(0, width - w.shape[-1])])


def _layer0_weights(w_in):
    parts, o = [], 0
    for s in AB_SIZES:
        parts.append(w_in[:, o:o + s])
        o += s
    qa, ka, va, iq, ik, iw, qb, kb, vb, glow, og = parts
    cols = [qa, ka, va, qb, kb, vb, og,
            iq, _pad_cols(ik, LANES), _pad_cols(iw, LANES), _pad_cols(glow, 2 * LANES)]
    return jnp.concatenate(cols, axis=1).astype(BF16)


def kernel(x, g_mix, g_ffn, w_in_ab, gq_a, gk_a, w_gate_up, b_gate, g_gla, w_out_ab,
           w_in_c, gq_c, gk_c, w_out_c, w_up, w_down):
    bsz, s_len, d = x.shape
    m = bsz * s_len
    h = x.reshape(m, d)

    qk_gains = jnp.stack([gq_a[0] * HEAD_DIM ** -0.5, gk_a[0]])
    rope_a = _rope_tables(s_len, HEAD_DIM, 1)
    rope_i = _rope_tables(s_len, IDX_DIM, LANES // IDX_DIM)
    *dsa_ops, qkvg, g_low = _proj0(h, g_mix[0], _layer0_weights(w_in_ab[0]), qk_gains, rope_a, rope_i,
                                   bsz, s_len, tm=PROJ_TM)
    oa = _dsa(*dsa_ops)
    wg = _gla_gate_weights(w_gate_up[0])
    bg = b_gate[0].reshape(1, N_HEADS_B * GLA_DK)
    ob = _gla(qkvg, g_low, wg, bg, g_gla[0])
    w_o = w_out_ab[0].astype(BF16)
    h = _proj_mlp(h, [oa.reshape(m, A_WIDTH), ob.reshape(m, B_WIDTH)], [w_o[:A_WIDTH], w_o[A_WIDTH:]],
                  g_ffn[0], w_up[0].astype(BF16), w_down[0].astype(BF16), tm=MLP_TM, tf=MLP_TF)

    qk_gains = jnp.stack([gq_c[0] * (HEAD_DIM ** -0.5 * LOG2E), gk_c[0]])
    qkv = _norm_matmul(h, g_mix[1], w_in_c[0].astype(BF16), qk_gains, tm=PROJ_TM, tn=C_WIDTH)
    oc = _stick_breaking(qkv.reshape(bsz, s_len, 3 * C_WIDTH))
    h = _proj_mlp(h, [oc.reshape(m, C_WIDTH)], [w_out_c[0].astype(BF16)],
                  g_ffn[1], w_up[1].astype(BF16), w_down[1].astype(BF16), tm=MLP_TM, tf=MLP_TF)
    return h.reshape(bsz, s_len, d)
```
